```python
import math
import jax
import jax.numpy as jnp
from jax import lax
import numpy as np

D_MODEL = 1024
BATCH = 4
SEQ = 4096
DEPTH = 4

GRID_W = 64
CTX_LEN = 256
N_BRANCH = 4
BRANCH_W = 512
FNET_GROUPS = 4
FNET_GW = BRANCH_W // FNET_GROUPS
DIFF_HEADS = 4
DIFF_HD = 64
DIFF_VD = 2 * DIFF_HD
POOL_WINDOWS = (2, 4, 8, 16)
POOL_GW = BRANCH_W // len(POOL_WINDOWS)
NA_HEADS = 8
NA_HD = BRANCH_W // NA_HEADS
NA_KH = 8
NA_KW = 16
Q_BLOCK = 128
ROPE_BASE = 10000.0
LN_EPS = 1e-6
SUBLN_EPS = 1e-5
SPLIT_SIZES = (BRANCH_W,) * 12 + (N_BRANCH * D_MODEL,)
PROJ_W = 12 * BRANCH_W + N_BRANCH * D_MODEL

kernel_name = 'hybrid_gated_parallel_mixer_dit'


def layer_norm(x, g=None, b=None):
    x32 = x.astype(jnp.float32)
    mu = jnp.mean(x32, axis=-1, keepdims=True)
    var = jnp.mean(jnp.square(x32 - mu), axis=-1, keepdims=True)
    y = (x32 - mu) * lax.rsqrt(var + LN_EPS)
    if g is not None:
        y = y * g.astype(jnp.float32) + b.astype(jnp.float32)
    return y.astype(x.dtype)


def split_proj(z):
    idx, acc = [], 0
    for s in SPLIT_SIZES[:-1]:
        acc += s
        idx.append(acc)
    return jnp.split(z, idx, axis=-1)


def to_heads(t, tail):
    return t.reshape(t.shape[:2] + tail)


def axial_rope_angles(n):
    t = jnp.arange(n)
    rows = (t // GRID_W).astype(jnp.float32)
    cols = (t % GRID_W).astype(jnp.float32)
    nf = DIFF_HD // 4
    inv = ROPE_BASE ** (-jnp.arange(nf, dtype=jnp.float32) / nf)
    return jnp.stack([rows[:, None] * inv, cols[:, None] * inv], axis=1)


def apply_rope(x, ang):
    nf = ang.shape[-1]
    xr = x.reshape(x.shape[:-1] + (2, 2, nf))
    shape = (1, ang.shape[0]) + (1,) * (x.ndim - 3) + (2, nf)
    cos = jnp.cos(ang).reshape(shape).astype(x.dtype)
    sin = jnp.sin(ang).reshape(shape).astype(x.dtype)
    x0 = xr[..., 0, :]
    x1 = xr[..., 1, :]
    out = jnp.stack([x0 * cos - x1 * sin, x1 * cos + x0 * sin], axis=-2)
    return out.reshape(x.shape)


def fourier_mix(u, w_grp):
    b, n, _ = u.shape
    ug = u.astype(jnp.float32).reshape(b, n, FNET_GROUPS, FNET_GW)
    f = jnp.fft.fft2(ug, axes=(1, 3), norm='ortho').real.astype(u.dtype)
    return jnp.einsum('bngc,gcd->bngd', f, w_grp).reshape(b, n, BRANCH_W)


def pool_mix(u, w_grp, scale):
    b, n, _ = u.shape
    u32 = u.astype(jnp.float32)
    cs = jnp.concatenate([jnp.zeros((b, 1, BRANCH_W), jnp.float32), jnp.cumsum(u32, axis=1)], axis=1)
    t = jnp.arange(n)
    means = []
    for g, w in enumerate(POOL_WINDOWS):
        lo = jnp.clip(t - w // 2, 0, n)
        hi = jnp.clip(t - w // 2 + w, 0, n)
        seg = cs[:, :, g * POOL_GW:(g + 1) * POOL_GW]
        means.append((seg[:, hi] - seg[:, lo]) / (hi - lo).astype(jnp.float32)[:, None])
    pooled = (jnp.concatenate(means, axis=-1) - u32).astype(u.dtype)
    pooled = pooled.reshape(b, n, len(POOL_WINDOWS), POOL_GW)
    y = jnp.einsum('bngc,gcd->bngd', pooled, w_grp).reshape(b, n, BRANCH_W)
    return y * scale


def diff_lambda(lam_vecs, lambda_init):
    lv = lam_vecs.astype(jnp.float32)
    return jnp.exp(jnp.sum(lv[0] * lv[1])) - jnp.exp(jnp.sum(lv[2] * lv[3])) + lambda_init


def diff_weights(q, k, lam):
    s = jnp.einsum('bqhcd,bkhcd->bhcqk', q, k).astype(jnp.float32) * (DIFF_HD ** -0.5)
    p = jax.nn.softmax(s, axis=-1)
    return p[:, :, 0] - lam * p[:, :, 1]


def diff_out(a, v, subln_w, lambda_init):
    o = jnp.einsum('bhqk,bkhe->bqhe', a.astype(v.dtype), v).astype(jnp.float32)
    o = o * lax.rsqrt(jnp.mean(jnp.square(o), axis=-1, keepdims=True) + SUBLN_EPS)
    o = o * subln_w.astype(jnp.float32) * (1.0 - lambda_init)
    return o.astype(v.dtype).reshape(o.shape[:2] + (BRANCH_W,))


def diff_attn_latent(q, k_all, v_all, lam, subln_w, lambda_init):
    b, n = q.shape[:2]
    nb = n // Q_BLOCK
    qb = jnp.moveaxis(q.reshape((b, nb, Q_BLOCK) + q.shape[2:]), 1, 0)

    def block(qi):
        return diff_out(diff_weights(qi, k_all, lam), v_all, subln_w, lambda_init)

    o = lax.map(block, qb)
    return jnp.moveaxis(o, 0, 1).reshape(b, n, BRANCH_W)


def softmax_attend(q, k, v):
    s = jnp.einsum('bqhd,bkhd->bhqk', q, k).astype(jnp.float32) * (q.shape[-1] ** -0.5)
    p = jax.nn.softmax(s, axis=-1).astype(v.dtype)
    o = jnp.einsum('bhqk,bkhd->bqhd', p, v)
    return o.reshape(o.shape[:2] + (-1,))


def na_latent(q, k, v, k_ctx, v_ctx, bias_tab):
    b, n = q.shape[:2]
    rows = n // GRID_W
    kh = min(NA_KH, rows)
    kw = min(NA_KW, GRID_W)
    qg = q.reshape(b, rows, GRID_W, NA_HEADS, NA_HD)
    kg = k.reshape(b, rows, GRID_W, NA_HEADS, NA_HD)
    vg = v.reshape(b, rows, GRID_W, NA_HEADS, NA_HD)
    col = jnp.arange(GRID_W)
    cstart = jnp.clip(col - kw // 2, 0, GRID_W - kw)
    col_idx = cstart[:, None] + jnp.arange(kw)[None, :]
    col_off = col_idx - col[:, None] + (NA_KW - 1)
    scale = NA_HD ** -0.5

    def row_block(args):
        r, q_r = args
        rs = jnp.clip(r - kh // 2, 0, rows - kh)
        k_win = lax.dynamic_slice_in_dim(kg, rs, kh, axis=1)[:, :, col_idx]
        v_win = lax.dynamic_slice_in_dim(vg, rs, kh, axis=1)[:, :, col_idx]
        row_off = rs + jnp.arange(kh) - r + (NA_KH - 1)
        bias = bias_tab[:, row_off[None, :, None], col_off[:, None, :]]
        s_win = jnp.einsum('bqhd,bkqjhd->bhqkj', q_r, k_win).astype(jnp.float32) * scale
        s_win = s_win + bias.astype(jnp.float32)[None]
        s_ctx = jnp.einsum('bqhd,bkhd->bhqk', q_r, k_ctx).astype(jnp.float32) * scale
        logits = jnp.concatenate([s_win.reshape(b, NA_HEADS, GRID_W, kh * kw), s_ctx], axis=-1)
        p = jax.nn.softmax(logits, axis=-1).astype(v.dtype)
        p_win = p[..., :kh * kw].reshape(b, NA_HEADS, GRID_W, kh, kw)
        return (jnp.einsum('bhqkj,bkqjhd->bqhd', p_win, v_win)
                + jnp.einsum('bhqk,bkhd->bqhd', p[..., kh * kw:], v_ctx))

    o = lax.map(row_block, (jnp.arange(rows), jnp.moveaxis(qg, 1, 0)))
    return jnp.moveaxis(o, 0, 1).reshape(b, n, BRANCH_W)


def gate_and_merge(z, ys, w_branch, w_out):
    gated = [y * jax.nn.silu(z[i]) for y, i in zip(ys, (1, 5, 7, 11))]
    b, n, _ = z[12].shape
    g = jax.nn.sigmoid(z[12].reshape(b, n, N_BRANCH, D_MODEL).astype(jnp.float32)).astype(z[12].dtype)
    proj = jnp.einsum('bnie,ied->bnid', jnp.stack(gated, axis=2), w_branch)
    return jnp.einsum('bnd,de->bne', jnp.sum(g * proj, axis=2), w_out)


def trunk_layer(x, ctx, c, c_ctx, w_mod, b_mod, w_in, b_in, fnet_w, diff_lam, diff_subln,
                pool_w, pool_scale, na_bias, w_branch, w_out, ln_g, ln_b, lambda_init, alpha, need_ctx):
    n = x.shape[1]
    mod_lat = jnp.dot(jax.nn.silu(c), w_mod) + b_mod
    mod_ctx = jnp.dot(jax.nn.silu(c_ctx), w_mod) + b_mod
    sh_l, sc_l, g_l = jnp.split(mod_lat[:, None, :], 3, axis=-1)
    sh_c, sc_c, g_c = jnp.split(mod_ctx, 3, axis=-1)
    z_lat = split_proj(jnp.dot(layer_norm(x) * (1.0 + sc_l) + sh_l, w_in) + b_in)
    z_ctx = split_proj(jnp.dot(layer_norm(ctx) * (1.0 + sc_c) + sh_c, w_in) + b_in)
    lam = diff_lambda(diff_lam, lambda_init)
    ang = axial_rope_angles(n)
    dk_c = to_heads(z_ctx[3], (DIFF_HEADS, 2, DIFF_HD))
    dv_c = to_heads(z_ctx[4], (DIFF_HEADS, DIFF_VD))
    nk_c = to_heads(z_ctx[9], (NA_HEADS, NA_HD))
    nv_c = to_heads(z_ctx[10], (NA_HEADS, NA_HD))
    dq_l = apply_rope(to_heads(z_lat[2], (DIFF_HEADS, 2, DIFF_HD)), ang)
    dk_l = apply_rope(to_heads(z_lat[3], (DIFF_HEADS, 2, DIFF_HD)), ang)
    dv_l = to_heads(z_lat[4], (DIFF_HEADS, DIFF_VD))
    y_f = fourier_mix(z_lat[0], fnet_w)
    y_d = diff_attn_latent(dq_l, jnp.concatenate([dk_c, dk_l], axis=1),
                           jnp.concatenate([dv_c, dv_l], axis=1), lam, diff_subln, lambda_init)
    y_p = pool_mix(z_lat[6], pool_w, pool_scale)
    y_n = na_latent(to_heads(z_lat[8], (NA_HEADS, NA_HD)), to_heads(z_lat[9], (NA_HEADS, NA_HD)),
                    to_heads(z_lat[10], (NA_HEADS, NA_HD)), nk_c, nv_c, na_bias)
    out_lat = gate_and_merge(z_lat, (y_f, y_d, y_p, y_n), w_branch, w_out)
    x_new = layer_norm(alpha * x + g_l * out_lat, ln_g, ln_b)
    if not need_ctx:
        return x_new, ctx
    y_fc = fourier_mix(z_ctx[0], fnet_w)
    y_dc = diff_out(diff_weights(to_heads(z_ctx[2], (DIFF_HEADS, 2, DIFF_HD)), dk_c, lam),
                    dv_c, diff_subln, lambda_init)
    y_pc = pool_mix(z_ctx[6], pool_w, pool_scale)
    y_nc = softmax_attend(to_heads(z_ctx[8], (NA_HEADS, NA_HD)), nk_c, nv_c)
    out_ctx = gate_and_merge(z_ctx, (y_fc, y_dc, y_pc, y_nc), w_branch, w_out)
    ctx_new = layer_norm(alpha * ctx + g_c * out_ctx, ln_g, ln_b)
    return x_new, ctx_new


def setup_inputs(seed: int = 0) -> dict:
    key = jax.random.key(seed)
    ks = jax.random.split(key, 18)
    beta = (8.0 * DEPTH) ** -0.25

    def nrm(k, shape, s):
        return jax.random.normal(k, shape, jnp.float32) * s

    return {
        'x': nrm(ks[0], (BATCH, SEQ, D_MODEL), 1.0),
        'c': nrm(ks[1], (BATCH, D_MODEL), 1.0),
        'ctx': nrm(ks[2], (BATCH, CTX_LEN, D_MODEL), 1.0),
        'c_ctx': nrm(ks[3], (D_MODEL,), 1.0),
        'w_mod': nrm(ks[4], (DEPTH, D_MODEL, 3 * D_MODEL), 0.5 * D_MODEL ** -0.5),
        'b_mod': nrm(ks[5], (DEPTH, 3 * D_MODEL), 0.01),
        'w_in': nrm(ks[6], (DEPTH, D_MODEL, PROJ_W), D_MODEL ** -0.5),
        'b_in': nrm(ks[7], (DEPTH, PROJ_W), 0.01),
        'fnet_w': nrm(ks[8], (DEPTH, FNET_GROUPS, FNET_GW, FNET_GW), FNET_GW ** -0.5),
        'diff_lam': nrm(ks[9], (DEPTH, 4, DIFF_HD), 0.1),
        'diff_subln': 1.0 + nrm(ks[10], (DEPTH, DIFF_VD), 0.02),
        'pool_w': nrm(ks[11], (DEPTH, len(POOL_WINDOWS), POOL_GW, POOL_GW), POOL_GW ** -0.5),
        'pool_scale': 1.0 + nrm(ks[12], (DEPTH, BRANCH_W), 0.02),
        'na_bias': nrm(ks[13], (DEPTH, NA_HEADS, 2 * NA_KH - 1, 2 * NA_KW - 1), 0.1),
        'w_branch': nrm(ks[14], (DEPTH, N_BRANCH, BRANCH_W, D_MODEL), beta * BRANCH_W ** -0.5),
        'w_out': nrm(ks[15], (DEPTH, D_MODEL, D_MODEL), beta * D_MODEL ** -0.5),
        'ln_g': 1.0 + nrm(ks[16], (DEPTH, D_MODEL), 0.02),
        'ln_b': nrm(ks[17], (DEPTH, D_MODEL), 0.01),
    }


def reference(x, c, ctx, c_ctx, w_mod, b_mod, w_in, b_in, fnet_w, diff_lam, diff_subln,
              pool_w, pool_scale, na_bias, w_branch, w_out, ln_g, ln_b):
    alpha = (2.0 * DEPTH) ** 0.25
    h, hc = x, ctx
    for l in range(DEPTH):
        lambda_init = 0.8 - 0.6 * math.exp(-0.3 * l)
        h, hc = trunk_layer(h, hc, c, c_ctx, w_mod[l], b_mod[l], w_in[l], b_in[l], fnet_w[l],
                            diff_lam[l], diff_subln[l], pool_w[l], pool_scale[l], na_bias[l],
                            w_branch[l], w_out[l], ln_g[l], ln_b[l], lambda_init, alpha,
                            l < DEPTH - 1)
    return h
```

```python
import functools
import math

import numpy as np
import jax
import jax.numpy as jnp
from jax import lax
from jax.experimental import pallas as pl
from jax.experimental.pallas import tpu as pltpu

D_MODEL = 1024
DEPTH = 4
GRID_W = 64
N_BRANCH = 4
BRANCH_W = 512
FNET_GROUPS = 4
FNET_GW = BRANCH_W // FNET_GROUPS
DIFF_HEADS = 4
DIFF_HD = 64
DIFF_VD = 2 * DIFF_HD
POOL_WINDOWS = (2, 4, 8, 16)
POOL_GW = BRANCH_W // len(POOL_WINDOWS)
NA_HEADS = 8
NA_HD = BRANCH_W // NA_HEADS
NA_KH = 8
NA_KW = 16
ROPE_BASE = 10000.0
LN_EPS = 1e-6
SUBLN_EPS = 1e-5

LANES = 128
POOL_PAD = 16
NA_ROWS_PER_BLOCK = 4
NA_KEY_ROWS = NA_ROWS_PER_BLOCK + NA_KH - 1
MASK_VALUE = -1e30
VMEM_LIMIT = 56 * 1024 * 1024

F32 = jnp.float32
BF16 = jnp.bfloat16


def _cparams(*sem):
    return pltpu.CompilerParams(dimension_semantics=sem, vmem_limit_bytes=VMEM_LIMIT)


def _dot(a, b):
    return jnp.dot(a, b, preferred_element_type=F32)


def _dot_nt(a, b):
    return lax.dot_general(a, b, (((1,), (1,)), ((), ())), preferred_element_type=F32)


def _sigmoid(x):
    return 1.0 / (1.0 + jnp.exp(-x))


def _modulated_ln(x, mod_ref):
    mu = jnp.mean(x, axis=-1, keepdims=True)
    xc = x - mu
    var = jnp.mean(xc * xc, axis=-1, keepdims=True)
    y = xc * lax.rsqrt(var + LN_EPS)
    return y * (1.0 + mod_ref[0, 1:2, :]) + mod_ref[0, 0:1, :]


def _mod_kernel(c_ref, w_ref, b_ref, o_ref):
    c = c_ref[...]
    s = (c * _sigmoid(c)).astype(BF16)
    o_ref[0] = _dot(s, w_ref[0].astype(BF16)) + b_ref[0]


def _modulation(cc, w_mod, b_mod):
    n_l, d, d3 = w_mod.shape
    r = cc.shape[0]
    tn = d
    return pl.pallas_call(
        _mod_kernel,
        grid=(n_l, d3 // tn),
        in_specs=[
            pl.BlockSpec((r, d), lambda l, j: (0, 0)),
            pl.BlockSpec((1, d, tn), lambda l, j: (l, 0, j)),
            pl.BlockSpec((1, 1, tn), lambda l, j: (l, 0, j)),
        ],
        out_specs=pl.BlockSpec((1, r, tn), lambda l, j: (l, 0, j)),
        out_shape=jax.ShapeDtypeStruct((n_l, r, d3), F32),
        compiler_params=_cparams("arbitrary", "arbitrary"),
        name="modulation",
    )(cc, w_mod, b_mod.reshape(n_l, 1, d3))


def _rope(r, cos_ref, sin_ref):
    lane = lax.broadcasted_iota(jnp.int32, (r.shape[0], LANES), 1)
    first = (lane % 32) < 16
    cos = cos_ref[...]
    sin = sin_ref[...]
    outs = []
    for k in range(r.shape[1] // LANES):
        xk = r[:, k * LANES:(k + 1) * LANES]
        partner = jnp.where(first, pltpu.roll(xk, LANES - 16, 1), pltpu.roll(xk, 16, 1))
        outs.append(xk * cos + partner * sin)
    return jnp.concatenate(outs, axis=1)


def _inproj_kernel(*refs, rope):
    if rope:
        x_ref, mod_ref, w_ref, b_ref, cos_ref, sin_ref, o32_ref, o16_ref = refs
    else:
        x_ref, mod_ref, w_ref, b_ref, o32_ref, o16_ref = refs
    xm = _modulated_ln(x_ref[0], mod_ref).astype(BF16)
    bw = BRANCH_W
    for j in range(2):
        o32_ref[0, :, j * bw:(j + 1) * bw] = _dot(xm, w_ref[:, j * bw:(j + 1) * bw]) + b_ref[:, j * bw:(j + 1) * bw]
    for j in range(6):
        c0 = (2 + j) * bw
        r = _dot(xm, w_ref[:, c0:c0 + bw]) + b_ref[:, c0:c0 + bw]
        if rope and j in (0, 1):
            r = _rope(r, cos_ref, sin_ref)
        o16_ref[0, :, j * bw:(j + 1) * bw] = r.astype(BF16)


def _inproj(x, mod, per_batch_mod, wa, ba, rope_tabs, tm):
    b, s, d = x.shape
    na = wa.shape[1]
    mod_map = (lambda bi, i: (bi, 0, 0)) if per_batch_mod else (lambda bi, i: (0, 0, 0))
    in_specs = [
        pl.BlockSpec((1, tm, d), lambda bi, i: (bi, i, 0)),
        pl.BlockSpec((1, 3, d), mod_map),
        pl.BlockSpec((d, na), lambda bi, i: (0, 0)),
        pl.BlockSpec((1, na), lambda bi, i: (0, 0)),
    ]
    args = [x, mod, wa, ba]
    if rope_tabs is not None:
        in_specs += [pl.BlockSpec((tm, LANES), lambda bi, i: (i, 0))] * 2
        args += list(rope_tabs)
    return pl.pallas_call(
        functools.partial(_inproj_kernel, rope=rope_tabs is not None),
        grid=(b, s // tm),
        in_specs=in_specs,
        out_specs=[
            pl.BlockSpec((1, tm, 2 * BRANCH_W), lambda bi, i: (bi, i, 0)),
            pl.BlockSpec((1, tm, 6 * BRANCH_W), lambda bi, i: (bi, i, 0)),
        ],
        out_shape=[
            jax.ShapeDtypeStruct((b, s, 2 * BRANCH_W), F32),
            jax.ShapeDtypeStruct((b, s, 6 * BRANCH_W), BF16),
        ],
        compiler_params=_cparams("arbitrary", "arbitrary"),
        name="inproj_rope" if rope_tabs is not None else "inproj",
    )(*args)


def _fnet_kernel(u_ref, cs_ref, cc_ref, w_ref, o_ref, v_ref, *, scale):
    s = u_ref.shape[1]
    gw = FNET_GW

    @pl.when(pl.program_id(1) == 0)
    def _():
        for g in range(FNET_GROUPS):
            ug = u_ref[0, :, g * gw:(g + 1) * gw].astype(BF16)
            t = _dot(ug, cc_ref[...])
            v_ref[0:s, g * gw:(g + 1) * gw] = t[:, :gw].astype(BF16)
            v_ref[s:2 * s, g * gw:(g + 1) * gw] = t[:, gw:].astype(BF16)

    f = _dot(cs_ref[...], v_ref[...]) * scale
    for g in range(FNET_GROUPS):
        o_ref[0, :, g * gw:(g + 1) * gw] = _dot(f[:, g * gw:(g + 1) * gw].astype(BF16), w_ref[g])


def _fnet(o32, cs, ccsc, w, tn):
    b, s, _ = o32.shape
    scale = 1.0 / math.sqrt(s * FNET_GW)
    return pl.pallas_call(
        functools.partial(_fnet_kernel, scale=scale),
        grid=(b, s // tn),
        in_specs=[
            pl.BlockSpec((1, s, BRANCH_W), lambda bi, i: (bi, 0, 0)),
            pl.BlockSpec((tn, 2 * s), lambda bi, i: (i, 0)),
            pl.BlockSpec((FNET_GW, 2 * FNET_GW), lambda bi, i: (0, 0)),
            pl.BlockSpec((FNET_GROUPS, FNET_GW, FNET_GW), lambda bi, i: (0, 0, 0)),
        ],
        out_specs=pl.BlockSpec((1, tn, BRANCH_W), lambda bi, i: (bi, i, 0)),
        out_shape=jax.ShapeDtypeStruct((b, s, BRANCH_W), F32),
        scratch_shapes=[pltpu.VMEM((2 * s, BRANCH_W), BF16)],
        compiler_params=_cparams("arbitrary", "arbitrary"),
        name="fnet",
    )(o32, cs, ccsc, w)


def _pool_kernel(u_ref, w_ref, sc_ref, o_ref, pad_ref):
    s = u_ref.shape[1]
    gw = POOL_GW
    t = lax.broadcasted_iota(jnp.int32, (s, 1), 0)
    zeros = jnp.zeros((POOL_PAD, gw), F32)
    for g, win in enumerate(POOL_WINDOWS):
        u = u_ref[0, :, g * gw:(g + 1) * gw]
        pad_ref[0:POOL_PAD, :] = zeros
        pad_ref[POOL_PAD + s:2 * POOL_PAD + s, :] = zeros
        pad_ref[POOL_PAD:POOL_PAD + s, :] = u
        acc = None
        for j in range(-(win // 2), win - win // 2):
            term = pad_ref[POOL_PAD + j:POOL_PAD + j + s, :]
            acc = term if acc is None else acc + term
        lo = jnp.clip(t - win // 2, 0, s)
        hi = jnp.clip(t - win // 2 + win, 0, s)
        cnt = (hi - lo).astype(F32)
        pooled = (acc / cnt - u).astype(BF16)
        y = _dot(pooled, w_ref[g])
        o_ref[0, :, g * gw:(g + 1) * gw] = y * sc_ref[:, g * gw:(g + 1) * gw]


def _pool(o32, w, scale):
    b, s, _ = o32.shape
    return pl.pallas_call(
        _pool_kernel,
        grid=(b,),
        in_specs=[
            pl.BlockSpec((1, s, BRANCH_W), lambda bi: (bi, 0, 1)),
            pl.BlockSpec((len(POOL_WINDOWS), POOL_GW, POOL_GW), lambda bi: (0, 0, 0)),
            pl.BlockSpec((1, BRANCH_W), lambda bi: (0, 0)),
        ],
        out_specs=pl.BlockSpec((1, s, BRANCH_W), lambda bi: (bi, 0, 0)),
        out_shape=jax.ShapeDtypeStruct((b, s, BRANCH_W), F32),
        scratch_shapes=[pltpu.VMEM((s + 2 * POOL_PAD, POOL_GW), F32)],
        compiler_params=_cparams("arbitrary"),
        name="pool",
    )(o32, w, scale)


def _softmax_parts(scores):
    m = functools.reduce(jnp.maximum, [jnp.max(s, axis=-1, keepdims=True) for s in scores])
    es = [jnp.exp(s - m) for s in scores]
    den = functools.reduce(lambda a, c: a + c, [jnp.sum(e, axis=-1, keepdims=True) for e in es])
    return es, 1.0 / den


def _diff_kernel(*refs, n_seg, lambda_init):
    q_ref = refs[0]
    k_refs = refs[1:1 + n_seg]
    v_refs = refs[1 + n_seg:1 + 2 * n_seg]
    lam_ref, sub_ref, o_ref = refs[1 + 2 * n_seg:]
    lv = lam_ref[...]
    lam = (jnp.exp(jnp.sum(lv[0:1] * lv[1:2], axis=-1, keepdims=True))
           - jnp.exp(jnp.sum(lv[2:3] * lv[3:4], axis=-1, keepdims=True)) + lambda_init)
    q = q_ref[0]
    lane = lax.broadcasted_iota(jnp.int32, q.shape, 1)
    zero = jnp.zeros_like(q)
    scale = DIFF_HD ** -0.5
    parts = []
    for comp in range(2):
        in_comp = (lane < DIFF_HD) if comp == 0 else (lane >= DIFF_HD)
        qc = jnp.where(in_comp, q, zero)
        es, rden = _softmax_parts([_dot_nt(qc, k_ref[0]) * scale for k_ref in k_refs])
        parts.append((es, rden))
    (e1, r1), (e2, r2) = parts
    r2 = r2 * lam
    o = None
    for seg in range(n_seg):
        a = (e1[seg] * r1 - e2[seg] * r2).astype(BF16)
        pv = _dot(a, v_refs[seg][0])
        o = pv if o is None else o + pv
    o = o * lax.rsqrt(jnp.mean(o * o, axis=-1, keepdims=True) + SUBLN_EPS)
    o_ref[0] = o * sub_ref[...] * (1.0 - lambda_init)


def _diff_attn(q16, kv16_list, lam_vecs, subln, lambda_init, tq):
    b, sq, _ = q16.shape
    n_seg = len(kv16_list)
    nh = DIFF_HEADS
    in_specs = [pl.BlockSpec((1, tq, LANES), lambda bi, h, i: (bi, i, h))]
    in_specs += [pl.BlockSpec((1, kv.shape[1], LANES), lambda bi, h, i: (bi, 0, nh + h)) for kv in kv16_list]
    in_specs += [pl.BlockSpec((1, kv.shape[1], LANES), lambda bi, h, i: (bi, 0, 2 * nh + h)) for kv in kv16_list]
    in_specs += [
        pl.BlockSpec((4, DIFF_HD), lambda bi, h, i: (0, 0)),
        pl.BlockSpec((1, DIFF_VD), lambda bi, h, i: (0, 0)),
    ]
    return pl.pallas_call(
        functools.partial(_diff_kernel, n_seg=n_seg, lambda_init=lambda_init),
        grid=(b, nh, sq // tq),
        in_specs=in_specs,
        out_specs=pl.BlockSpec((1, tq, LANES), lambda bi, h, i: (bi, i, h)),
        out_shape=jax.ShapeDtypeStruct((b, sq, BRANCH_W), F32),
        compiler_params=_cparams("arbitrary", "arbitrary", "arbitrary"),
        name="diff_attn",
    )(q16, *kv16_list, *kv16_list, lam_vecs, subln)


NA_Q_BLK = 3 * BRANCH_W // LANES
NA_K_BLK = 4 * BRANCH_W // LANES
NA_V_BLK = 5 * BRANCH_W // LANES


def _na_head_pair(q, segs, o_ref):
    lane = lax.broadcasted_iota(jnp.int32, q.shape, 1)
    zero = jnp.zeros_like(q)
    scale = NA_HD ** -0.5
    outs = []
    for hh in range(2):
        in_head = (lane < NA_HD) if hh == 0 else (lane >= NA_HD)
        qh = jnp.where(in_head, q, zero)
        scores = []
        for k, _, bias in segs:
            sc = _dot_nt(qh, k) * scale
            if bias is not None:
                sc = sc + bias[0, hh]
            scores.append(sc)
        es, rden = _softmax_parts(scores)
        o = None
        for (_, v, _), e in zip(segs, es):
            pv = _dot((e * rden).astype(BF16), v)
            o = pv if o is None else o + pv
        outs.append(o)
    lane_o = lax.broadcasted_iota(jnp.int32, outs[0].shape, 1)
    o_ref[0] = jnp.where(lane_o < NA_HD, outs[0], outs[1])


def _na_lat_kernel(q_ref, k_ref, v_ref, kc_ref, vc_ref, bias_ref, o_ref, *, rows):
    blk = pl.program_id(2)
    kstart = jnp.clip(blk * NA_ROWS_PER_BLOCK - NA_KH // 2, 0, rows - NA_KEY_ROWS)
    off = pl.multiple_of(kstart * GRID_W, GRID_W)
    nk = NA_KEY_ROWS * GRID_W
    kw = k_ref[0, pl.ds(off, nk), :]
    vw = v_ref[0, pl.ds(off, nk), :]
    _na_head_pair(q_ref[0], [(kw, vw, bias_ref), (kc_ref[0], vc_ref[0], None)], o_ref)


def _na_lat(lat16, ctx16, bias):
    b, s, _ = lat16.shape
    sc = ctx16.shape[1]
    rows = s // GRID_W
    nblk = rows // NA_ROWS_PER_BLOCK
    tq = NA_ROWS_PER_BLOCK * GRID_W
    nk = NA_KEY_ROWS * GRID_W

    def bias_map(bi, hp, blk):
        case = jnp.where(blk == 0, 0, jnp.where(blk == nblk - 1, 2, 1))
        return (case, hp, 0, 0)

    return pl.pallas_call(
        functools.partial(_na_lat_kernel, rows=rows),
        grid=(b, NA_HEADS // 2, nblk),
        in_specs=[
            pl.BlockSpec((1, tq, LANES), lambda bi, hp, blk: (bi, blk, NA_Q_BLK + hp)),
            pl.BlockSpec((1, s, LANES), lambda bi, hp, blk: (bi, 0, NA_K_BLK + hp)),
            pl.BlockSpec((1, s, LANES), lambda bi, hp, blk: (bi, 0, NA_V_BLK + hp)),
            pl.BlockSpec((1, sc, LANES), lambda bi, hp, blk: (bi, 0, NA_K_BLK + hp)),
            pl.BlockSpec((1, sc, LANES), lambda bi, hp, blk: (bi, 0, NA_V_BLK + hp)),
            pl.BlockSpec((1, 2, tq, nk), bias_map),
        ],
        out_specs=pl.BlockSpec((1, tq, LANES), lambda bi, hp, blk: (bi, blk, hp)),
        out_shape=jax.ShapeDtypeStruct((b, s, BRANCH_W), F32),
        compiler_params=_cparams("arbitrary", "arbitrary", "arbitrary"),
        name="na_lat",
    )(lat16, lat16, lat16, ctx16, ctx16, bias)


def _na_ctx_kernel(q_ref, k_ref, v_ref, o_ref):
    _na_head_pair(q_ref[0], [(k_ref[0], v_ref[0], None)], o_ref)


def _na_ctx(ctx16):
    b, sc, _ = ctx16.shape
    return pl.pallas_call(
        _na_ctx_kernel,
        grid=(b, NA_HEADS // 2),
        in_specs=[
            pl.BlockSpec((1, sc, LANES), lambda bi, hp: (bi, 0, NA_Q_BLK + hp)),
            pl.BlockSpec((1, sc, LANES), lambda bi, hp: (bi, 0, NA_K_BLK + hp)),
            pl.BlockSpec((1, sc, LANES), lambda bi, hp: (bi, 0, NA_V_BLK + hp)),
        ],
        out_specs=pl.BlockSpec((1, sc, LANES), lambda bi, hp: (bi, 0, hp)),
        out_shape=jax.ShapeDtypeStruct((b, sc, BRANCH_W), F32),
        compiler_params=_cparams("arbitrary", "arbitrary"),
        name="na_ctx",
    )(ctx16, ctx16, ctx16)


def _na_bias_index(rows):
    r_blk = NA_ROWS_PER_BLOCK
    cases = [(0, 0), (r_blk, r_blk - NA_KH // 2), (rows - r_blk, rows - NA_KEY_ROWS)]
    qr = np.repeat(np.arange(r_blk), GRID_W)
    qc = np.tile(np.arange(GRID_W), r_blk)
    kr = np.repeat(np.arange(NA_KEY_ROWS), GRID_W)
    kc = np.tile(np.arange(GRID_W), NA_KEY_ROWS)
    row_idx, col_idx, valid = [], [], []
    for r0, kstart in cases:
        r = r0 + qr[:, None]
        krow = kstart + kr[None, :]
        rs = np.clip(r - NA_KH // 2, 0, rows - NA_KH)
        cs = np.clip(qc[:, None] - NA_KW // 2, 0, GRID_W - NA_KW)
        ok = (krow >= rs) & (krow < rs + NA_KH) & (kc[None, :] >= cs) & (kc[None, :] < cs + NA_KW)
        ro = np.clip(krow - r + NA_KH - 1, 0, 2 * NA_KH - 2)
        co = np.clip(kc[None, :] - qc[:, None] + NA_KW - 1, 0, 2 * NA_KW - 2)
        row_idx.append(ro)
        col_idx.append(co)
        valid.append(ok)
    return np.stack(row_idx), np.stack(col_idx), np.stack(valid)


def _na_bias(tab, rows):
    ro, co, ok = _na_bias_index(rows)
    g = tab[:, ro, co]
    g = jnp.where(ok[None], g, MASK_VALUE)
    return jnp.transpose(g, (1, 0, 2, 3))


def _final_kernel(x_ref, mod_ref, yf_ref, yd_ref, yp_ref, yn_ref, wg_ref, bg_ref, wb_ref, wo_ref,
                  g_ref, b_ref, o_ref, *, alpha):
    x = x_ref[0]
    xm = _modulated_ln(x, mod_ref).astype(BF16)
    bw = BRANCH_W
    d = D_MODEL
    acc = None
    for i, y_ref in enumerate((yf_ref, yd_ref, yp_ref, yn_ref)):
        zg = _dot(xm, wg_ref[:, i * bw:(i + 1) * bw]) + bg_ref[:, i * bw:(i + 1) * bw]
        gated = (y_ref[0] * (zg * _sigmoid(zg))).astype(BF16)
        proj = _dot(gated, wb_ref[i])
        c0 = N_BRANCH * bw + i * d
        zm = _dot(xm, wg_ref[:, c0:c0 + d]) + bg_ref[:, c0:c0 + d]
        term = _sigmoid(zm) * proj
        acc = term if acc is None else acc + term
    out = _dot(acc.astype(BF16), wo_ref[...])
    h = alpha * x + mod_ref[0, 2:3, :] * out
    mu = jnp.mean(h, axis=-1, keepdims=True)
    hc = h - mu
    var = jnp.mean(hc * hc, axis=-1, keepdims=True)
    o_ref[0] = hc * lax.rsqrt(var + LN_EPS) * g_ref[...] + b_ref[...]


def _final(x, mod, per_batch_mod, ys, wg, bg, wb, wo, ln_g, ln_b, alpha, tm):
    b, s, d = x.shape
    ng = wg.shape[1]
    mod_map = (lambda bi, i: (bi, 0, 0)) if per_batch_mod else (lambda bi, i: (0, 0, 0))
    tile = lambda w: pl.BlockSpec((1, tm, w), lambda bi, i: (bi, i, 0))
    const2 = lambda shape: pl.BlockSpec(shape, lambda bi, i: (0, 0), pipeline_mode=pl.Buffered(1))
    return pl.pallas_call(
        functools.partial(_final_kernel, alpha=alpha),
        grid=(b, s // tm),
        in_specs=[
            tile(d),
            pl.BlockSpec((1, 3, d), mod_map),
            tile(BRANCH_W), tile(BRANCH_W), tile(BRANCH_W), tile(BRANCH_W),
            const2((d, ng)),
            const2((1, ng)),
            pl.BlockSpec((N_BRANCH, BRANCH_W, d), lambda bi, i: (0, 0, 0), pipeline_mode=pl.Buffered(1)),
            const2((d, d)),
            const2((1, d)),
            const2((1, d)),
        ],
        out_specs=tile(d),
        out_shape=jax.ShapeDtypeStruct((b, s, d), F32),
        compiler_params=_cparams("arbitrary", "arbitrary"),
        name="gate_merge_out",
    )(x, mod, *ys, wg, bg, wb, wo, ln_g, ln_b)


def _rope_tables(s):
    t = jnp.arange(s)
    rows = (t // GRID_W).astype(F32)
    cols = (t % GRID_W).astype(F32)
    nf = DIFF_HD // 4
    inv = ROPE_BASE ** (-jnp.arange(nf, dtype=F32) / nf)
    ang = jnp.stack([rows[:, None] * inv, cols[:, None] * inv], axis=1)
    cos = jnp.cos(ang)
    sin = jnp.sin(ang)
    cos64 = jnp.concatenate([cos[:, 0], cos[:, 0], cos[:, 1], cos[:, 1]], axis=-1)
    sin64 = jnp.concatenate([-sin[:, 0], sin[:, 0], -sin[:, 1], sin[:, 1]], axis=-1)
    rep = LANES // DIFF_HD
    return jnp.tile(cos64, (1, rep)), jnp.tile(sin64, (1, rep))


def _dft_tables(n):
    idx = jnp.arange(n, dtype=jnp.int32)
    m = (idx[:, None] * idx[None, :]) % n
    ang = m.astype(F32) * (2.0 * math.pi / n)
    return jnp.cos(ang), jnp.sin(ang)


def _split_cols(w, sizes):
    out, acc = [], 0
    for sz in sizes:
        out.append(w[..., acc:acc + sz])
        acc += sz
    return out


def kernel(x, c, ctx, c_ctx, w_mod, b_mod, w_in, b_in, fnet_w, diff_lam, diff_subln, pool_w, pool_scale,
           na_bias, w_branch, w_out, ln_g, ln_b):
    bsz, s, d = x.shape
    s_ctx = ctx.shape[1]
    rows = s // GRID_W
    assert d == D_MODEL and s % (GRID_W * NA_ROWS_PER_BLOCK) == 0 and rows >= NA_KEY_ROWS
    alpha = (2.0 * DEPTH) ** 0.25

    z = _split_cols(w_in, (BRANCH_W,) * 12 + (N_BRANCH * D_MODEL,))
    zb = _split_cols(b_in, (BRANCH_W,) * 12 + (N_BRANCH * D_MODEL,))
    order_a = (0, 6, 2, 3, 4, 8, 9, 10)
    order_g = (1, 5, 7, 11, 12)
    wa = jnp.concatenate([z[i] for i in order_a], axis=-1).astype(BF16)
    ba = jnp.concatenate([zb[i] for i in order_a], axis=-1)
    wg = jnp.concatenate([z[i] for i in order_g], axis=-1).astype(BF16)
    bg = jnp.concatenate([zb[i] for i in order_g], axis=-1)
    wb16 = w_branch.astype(BF16)
    wo16 = w_out.astype(BF16)
    fw16 = fnet_w.astype(BF16)
    pw16 = pool_w.astype(BF16)

    rope_tabs = _rope_tables(s)
    cs_tabs = {}
    for n in {s, s_ctx}:
        cn, sn = _dft_tables(n)
        cs_tabs[n] = jnp.concatenate([cn, -sn], axis=1).astype(BF16)
    cc, sc_ = _dft_tables(FNET_GW)
    ccsc = jnp.concatenate([cc, sc_], axis=1).astype(BF16)

    r_pad = -(-(bsz + 1) // 8) * 8
    cc_all = jnp.zeros((r_pad, d), F32).at[:bsz].set(c).at[bsz].set(c_ctx)
    mod_all = _modulation(cc_all, w_mod, b_mod)

    tm_lat = min(512, s)
    tm_fin = min(256, s)
    tq_diff = min(256, s)
    tn_fnet = min(512, s)

    h, hc = x, ctx
    for l in range(DEPTH):
        lambda_init = 0.8 - 0.6 * math.exp(-0.3 * l)
        need_ctx = l < DEPTH - 1
        mod_lat = mod_all[l, :bsz].reshape(bsz, 3, d)
        mod_ctx = mod_all[l, bsz:bsz + 1].reshape(1, 3, d)

        lat32, lat16 = _inproj(h, mod_lat, True, wa[l], ba[l][None], rope_tabs, tm_lat)
        ctx32, ctx16 = _inproj(hc, mod_ctx, False, wa[l], ba[l][None], None, s_ctx)

        lam = diff_lam[l]
        sub = diff_subln[l][None]
        psc = pool_scale[l][None]
        y_f = _fnet(lat32, cs_tabs[s], ccsc, fw16[l], tn_fnet)
        y_d = _diff_attn(lat16, [ctx16, lat16], lam, sub, lambda_init, tq_diff)
        y_p = _pool(lat32, pw16[l], psc)
        y_n = _na_lat(lat16, ctx16, _na_bias(na_bias[l], rows))
        h_new = _final(h, mod_lat, True, (y_f, y_d, y_p, y_n), wg[l], bg[l][None], wb16[l], wo16[l],
                       ln_g[l][None], ln_b[l][None], alpha, tm_fin)
        if need_ctx:
            y_fc = _fnet(ctx32, cs_tabs[s_ctx], ccsc, fw16[l], s_ctx)
            y_dc = _diff_attn(ctx16, [ctx16], lam, sub, lambda_init, s_ctx)
            y_pc = _pool(ctx32, pw16[l], psc)
            y_nc = _na_ctx(ctx16)
            hc = _final(hc, mod_ctx, False, (y_fc, y_dc, y_pc, y_nc), wg[l], bg[l][None], wb16[l], wo16[l],
                        ln_g[l][None], ln_b[l][None], alpha, s_ctx)
        h = h_new
    return h
```

```python
import functools
import math

import numpy as np
import jax
import jax.numpy as jnp
from jax import lax
from jax.experimental import pallas as pl
from jax.experimental.pallas import tpu as pltpu

D_MODEL = 1024
DEPTH = 4
GRID_W = 64
N_BRANCH = 4
BRANCH_W = 512
FNET_GROUPS = 4
FNET_GW = BRANCH_W // FNET_GROUPS
DIFF_HEADS = 4
DIFF_HD = 64
DIFF_VD = 2 * DIFF_HD
POOL_WINDOWS = (2, 4, 8, 16)
POOL_GW = BRANCH_W // len(POOL_WINDOWS)
NA_HEADS = 8
NA_HD = BRANCH_W // NA_HEADS
NA_KH = 8
NA_KW = 16
ROPE_BASE = 10000.0
LN_EPS = 1e-6
SUBLN_EPS = 1e-5

LANES = 128
POOL_PAD = 16
NA_ROWS_PER_BLOCK = 4
NA_KEY_ROWS = NA_ROWS_PER_BLOCK + NA_KH - 1
MASK_VALUE = -1e30
VMEM_LIMIT = 56 * 1024 * 1024

F32 = jnp.float32
BF16 = jnp.bfloat16


def _cparams(*sem):
    return pltpu.CompilerParams(dimension_semantics=sem, vmem_limit_bytes=VMEM_LIMIT)


def _dot(a, b):
    return jnp.dot(a, b, preferred_element_type=F32)


def _dot_nt(a, b):
    return lax.dot_general(a, b, (((1,), (1,)), ((), ())), preferred_element_type=F32)


def _sigmoid(x):
    return 1.0 / (1.0 + jnp.exp(-x))


def _modulated_ln(x, mod_ref):
    mu = jnp.mean(x, axis=-1, keepdims=True)
    xc = x - mu
    var = jnp.mean(xc * xc, axis=-1, keepdims=True)
    y = xc * lax.rsqrt(var + LN_EPS)
    return y * (1.0 + mod_ref[0, 1:2, :]) + mod_ref[0, 0:1, :]


def _mod_kernel(c_ref, w_ref, b_ref, o_ref):
    c = c_ref[...]
    s = (c * _sigmoid(c)).astype(BF16)
    o_ref[0] = _dot(s, w_ref[0].astype(BF16)) + b_ref[0]


def _modulation(cc, w_mod, b_mod):
    n_l, d, d3 = w_mod.shape
    r = cc.shape[0]
    tn = d
    return pl.pallas_call(
        _mod_kernel,
        grid=(n_l, d3 // tn),
        in_specs=[
            pl.BlockSpec((r, d), lambda l, j: (0, 0)),
            pl.BlockSpec((1, d, tn), lambda l, j: (l, 0, j)),
            pl.BlockSpec((1, 1, tn), lambda l, j: (l, 0, j)),
        ],
        out_specs=pl.BlockSpec((1, r, tn), lambda l, j: (l, 0, j)),
        out_shape=jax.ShapeDtypeStruct((n_l, r, d3), F32),
        compiler_params=_cparams("arbitrary", "arbitrary"),
        name="modulation",
    )(cc, w_mod, b_mod.reshape(n_l, 1, d3))


def _rope(r, cos_ref, sin_ref):
    lane = lax.broadcasted_iota(jnp.int32, (r.shape[0], LANES), 1)
    first = (lane % 32) < 16
    cos = cos_ref[...]
    sin = sin_ref[...]
    outs = []
    for k in range(r.shape[1] // LANES):
        xk = r[:, k * LANES:(k + 1) * LANES]
        partner = jnp.where(first, pltpu.roll(xk, LANES - 16, 1), pltpu.roll(xk, 16, 1))
        outs.append(xk * cos + partner * sin)
    return jnp.concatenate(outs, axis=1)


def _inproj_kernel(*refs, rope):
    if rope:
        x_ref, mod_ref, w_ref, b_ref, cos_ref, sin_ref, o32_ref, o16_ref = refs
    else:
        x_ref, mod_ref, w_ref, b_ref, o32_ref, o16_ref = refs
    xm = _modulated_ln(x_ref[0], mod_ref).astype(BF16)
    bw = BRANCH_W
    for j in range(2):
        o32_ref[0, :, j * bw:(j + 1) * bw] = _dot(xm, w_ref[:, j * bw:(j + 1) * bw]) + b_ref[:, j * bw:(j + 1) * bw]
    for j in range(6):
        c0 = (2 + j) * bw
        r = _dot(xm, w_ref[:, c0:c0 + bw]) + b_ref[:, c0:c0 + bw]
        if rope and j in (0, 1):
            r = _rope(r, cos_ref, sin_ref)
        o16_ref[0, :, j * bw:(j + 1) * bw] = r.astype(BF16)


def _inproj(x, mod, per_batch_mod, wa, ba, rope_tabs, tm):
    b, s, d = x.shape
    na = wa.shape[1]
    mod_map = (lambda bi, i: (bi, 0, 0)) if per_batch_mod else (lambda bi, i: (0, 0, 0))
    in_specs = [
        pl.BlockSpec((1, tm, d), lambda bi, i: (bi, i, 0)),
        pl.BlockSpec((1, 3, d), mod_map),
        pl.BlockSpec((d, na), lambda bi, i: (0, 0)),
        pl.BlockSpec((1, na), lambda bi, i: (0, 0)),
    ]
    args = [x, mod, wa, ba]
    if rope_tabs is not None:
        in_specs += [pl.BlockSpec((tm, LANES), lambda bi, i: (i, 0))] * 2
        args += list(rope_tabs)
    return pl.pallas_call(
        functools.partial(_inproj_kernel, rope=rope_tabs is not None),
        grid=(b, s // tm),
        in_specs=in_specs,
        out_specs=[
            pl.BlockSpec((1, tm, 2 * BRANCH_W), lambda bi, i: (bi, i, 0)),
            pl.BlockSpec((1, tm, 6 * BRANCH_W), lambda bi, i: (bi, i, 0)),
        ],
        out_shape=[
            jax.ShapeDtypeStruct((b, s, 2 * BRANCH_W), F32),
            jax.ShapeDtypeStruct((b, s, 6 * BRANCH_W), BF16),
        ],
        compiler_params=_cparams("arbitrary", "arbitrary"),
        name="inproj_rope" if rope_tabs is not None else "inproj",
    )(*args)


def _fnet_kernel(u_ref, cs_ref, cc_ref, w_ref, o_ref, v_ref, *, scale):
    s = u_ref.shape[1]
    gw = FNET_GW

    @pl.when(pl.program_id(1) == 0)
    def _():
        for g in range(FNET_GROUPS):
            ug = u_ref[0, :, g * gw:(g + 1) * gw].astype(BF16)
            t = _dot(ug, cc_ref[...])
            v_ref[0:s, g * gw:(g + 1) * gw] = t[:, :gw].astype(BF16)
            v_ref[s:2 * s, g * gw:(g + 1) * gw] = t[:, gw:].astype(BF16)

    f = _dot(cs_ref[...], v_ref[...]) * scale
    for g in range(FNET_GROUPS):
        o_ref[0, :, g * gw:(g + 1) * gw] = _dot(f[:, g * gw:(g + 1) * gw].astype(BF16), w_ref[g])


def _fnet(o32, cs, ccsc, w, tn):
    b, s, _ = o32.shape
    scale = 1.0 / math.sqrt(s * FNET_GW)
    return pl.pallas_call(
        functools.partial(_fnet_kernel, scale=scale),
        grid=(b, s // tn),
        in_specs=[
            pl.BlockSpec((1, s, BRANCH_W), lambda bi, i: (bi, 0, 0)),
            pl.BlockSpec((tn, 2 * s), lambda bi, i: (i, 0)),
            pl.BlockSpec((FNET_GW, 2 * FNET_GW), lambda bi, i: (0, 0)),
            pl.BlockSpec((FNET_GROUPS, FNET_GW, FNET_GW), lambda bi, i: (0, 0, 0)),
        ],
        out_specs=pl.BlockSpec((1, tn, BRANCH_W), lambda bi, i: (bi, i, 0)),
        out_shape=jax.ShapeDtypeStruct((b, s, BRANCH_W), F32),
        scratch_shapes=[pltpu.VMEM((2 * s, BRANCH_W), BF16)],
        compiler_params=_cparams("arbitrary", "arbitrary"),
        name="fnet",
    )(o32, cs, ccsc, w)


def _pool_kernel(u_ref, w_ref, sc_ref, o_ref, pad_ref):
    s = u_ref.shape[1]
    gw = POOL_GW
    t = lax.broadcasted_iota(jnp.int32, (s, 1), 0)
    zeros = jnp.zeros((POOL_PAD, gw), F32)
    for g, win in enumerate(POOL_WINDOWS):
        u = u_ref[0, :, g * gw:(g + 1) * gw]
        pad_ref[0:POOL_PAD, :] = zeros
        pad_ref[POOL_PAD + s:2 * POOL_PAD + s, :] = zeros
        pad_ref[POOL_PAD:POOL_PAD + s, :] = u
        acc = None
        for j in range(-(win // 2), win - win // 2):
            term = pad_ref[POOL_PAD + j:POOL_PAD + j + s, :]
            acc = term if acc is None else acc + term
        lo = jnp.clip(t - win // 2, 0, s)
        hi = jnp.clip(t - win // 2 + win, 0, s)
        cnt = (hi - lo).astype(F32)
        pooled = (acc / cnt - u).astype(BF16)
        y = _dot(pooled, w_ref[g])
        o_ref[0, :, g * gw:(g + 1) * gw] = y * sc_ref[:, g * gw:(g + 1) * gw]


def _pool(o32, w, scale):
    b, s, _ = o32.shape
    return pl.pallas_call(
        _pool_kernel,
        grid=(b,),
        in_specs=[
            pl.BlockSpec((1, s, BRANCH_W), lambda bi: (bi, 0, 1)),
            pl.BlockSpec((len(POOL_WINDOWS), POOL_GW, POOL_GW), lambda bi: (0, 0, 0)),
            pl.BlockSpec((1, BRANCH_W), lambda bi: (0, 0)),
        ],
        out_specs=pl.BlockSpec((1, s, BRANCH_W), lambda bi: (bi, 0, 0)),
        out_shape=jax.ShapeDtypeStruct((b, s, BRANCH_W), F32),
        scratch_shapes=[pltpu.VMEM((s + 2 * POOL_PAD, POOL_GW), F32)],
        compiler_params=_cparams("arbitrary"),
        name="pool",
    )(o32, w, scale)


def _softmax_parts(scores):
    m = functools.reduce(jnp.maximum, [jnp.max(s, axis=-1, keepdims=True) for s in scores])
    es = [jnp.exp(s - m) for s in scores]
    den = functools.reduce(lambda a, c: a + c, [jnp.sum(e, axis=-1, keepdims=True) for e in es])
    return es, 1.0 / den


def _diff_kernel(*refs, n_seg, lambda_init):
    q_ref = refs[0]
    k_refs = refs[1:1 + n_seg]
    v_refs = refs[1 + n_seg:1 + 2 * n_seg]
    lam_ref, sub_ref, o_ref = refs[1 + 2 * n_seg:]
    lv = lam_ref[...]
    lam = (jnp.exp(jnp.sum(lv[0:1] * lv[1:2], axis=-1, keepdims=True))
           - jnp.exp(jnp.sum(lv[2:3] * lv[3:4], axis=-1, keepdims=True)) + lambda_init)
    q = q_ref[0]
    lane = lax.broadcasted_iota(jnp.int32, q.shape, 1)
    zero = jnp.zeros_like(q)
    scale = DIFF_HD ** -0.5
    parts = []
    for comp in range(2):
        in_comp = (lane < DIFF_HD) if comp == 0 else (lane >= DIFF_HD)
        qc = jnp.where(in_comp, q, zero)
        es, rden = _softmax_parts([_dot_nt(qc, k_ref[0]) * scale for k_ref in k_refs])
        parts.append((es, rden))
    (e1, r1), (e2, r2) = parts
    r2 = r2 * lam
    o = None
    for seg in range(n_seg):
        a = (e1[seg] * r1 - e2[seg] * r2).astype(BF16)
        pv = _dot(a, v_refs[seg][0])
        o = pv if o is None else o + pv
    o = o * lax.rsqrt(jnp.mean(o * o, axis=-1, keepdims=True) + SUBLN_EPS)
    o_ref[0] = o * sub_ref[...] * (1.0 - lambda_init)


def _diff_attn(q16, kv16_list, lam_vecs, subln, lambda_init, tq):
    b, sq, _ = q16.shape
    n_seg = len(kv16_list)
    nh = DIFF_HEADS
    in_specs = [pl.BlockSpec((1, tq, LANES), lambda bi, h, i: (bi, i, h))]
    in_specs += [pl.BlockSpec((1, kv.shape[1], LANES), lambda bi, h, i: (bi, 0, nh + h)) for kv in kv16_list]
    in_specs += [pl.BlockSpec((1, kv.shape[1], LANES), lambda bi, h, i: (bi, 0, 2 * nh + h)) for kv in kv16_list]
    in_specs += [
        pl.BlockSpec((4, DIFF_HD), lambda bi, h, i: (0, 0)),
        pl.BlockSpec((1, DIFF_VD), lambda bi, h, i: (0, 0)),
    ]
    return pl.pallas_call(
        functools.partial(_diff_kernel, n_seg=n_seg, lambda_init=lambda_init),
        grid=(b, nh, sq // tq),
        in_specs=in_specs,
        out_specs=pl.BlockSpec((1, tq, LANES), lambda bi, h, i: (bi, i, h)),
        out_shape=jax.ShapeDtypeStruct((b, sq, BRANCH_W), F32),
        compiler_params=_cparams("arbitrary", "arbitrary", "arbitrary"),
        name="diff_attn",
    )(q16, *kv16_list, *kv16_list, lam_vecs, subln)


NA_Q_BLK = 3 * BRANCH_W // LANES
NA_K_BLK = 4 * BRANCH_W // LANES
NA_V_BLK = 5 * BRANCH_W // LANES


def _na_head_pair(q, segs, o_ref):
    lane = lax.broadcasted_iota(jnp.int32, q.shape, 1)
    zero = jnp.zeros_like(q)
    scale = NA_HD ** -0.5
    outs = []
    for hh in range(2):
        in_head = (lane < NA_HD) if hh == 0 else (lane >= NA_HD)
        qh = jnp.where(in_head, q, zero)
        scores = []
        for k, _, bias in segs:
            sc = _dot_nt(qh, k) * scale
            if bias is not None:
                sc = sc + bias[0, hh]
            scores.append(sc)
        es, rden = _softmax_parts(scores)
        o = None
        for (_, v, _), e in zip(segs, es):
            pv = _dot((e * rden).astype(BF16), v)
            o = pv if o is None else o + pv
        outs.append(o)
    lane_o = lax.broadcasted_iota(jnp.int32, outs[0].shape, 1)
    o_ref[0] = jnp.where(lane_o < NA_HD, outs[0], outs[1])


def _na_lat_kernel(q_ref, k_ref, v_ref, kc_ref, vc_ref, bias_ref, o_ref, *, rows):
    blk = pl.program_id(2)
    kstart = jnp.clip(blk * NA_ROWS_PER_BLOCK - NA_KH // 2, 0, rows - NA_KEY_ROWS)
    off = pl.multiple_of(kstart * GRID_W, GRID_W)
    nk = NA_KEY_ROWS * GRID_W
    kw = k_ref[0, pl.ds(off, nk), :]
    vw = v_ref[0, pl.ds(off, nk), :]
    _na_head_pair(q_ref[0], [(kw, vw, bias_ref), (kc_ref[0], vc_ref[0], None)], o_ref)


def _na_lat(lat16, ctx16, bias):
    b, s, _ = lat16.shape
    sc = ctx16.shape[1]
    rows = s // GRID_W
    nblk = rows // NA_ROWS_PER_BLOCK
    tq = NA_ROWS_PER_BLOCK * GRID_W
    nk = NA_KEY_ROWS * GRID_W

    def bias_map(bi, hp, blk):
        case = jnp.where(blk == 0, 0, jnp.where(blk == nblk - 1, 2, 1))
        return (case, hp, 0, 0)

    return pl.pallas_call(
        functools.partial(_na_lat_kernel, rows=rows),
        grid=(b, NA_HEADS // 2, nblk),
        in_specs=[
            pl.BlockSpec((1, tq, LANES), lambda bi, hp, blk: (bi, blk, NA_Q_BLK + hp)),
            pl.BlockSpec((1, s, LANES), lambda bi, hp, blk: (bi, 0, NA_K_BLK + hp)),
            pl.BlockSpec((1, s, LANES), lambda bi, hp, blk: (bi, 0, NA_V_BLK + hp)),
            pl.BlockSpec((1, sc, LANES), lambda bi, hp, blk: (bi, 0, NA_K_BLK + hp)),
            pl.BlockSpec((1, sc, LANES), lambda bi, hp, blk: (bi, 0, NA_V_BLK + hp)),
            pl.BlockSpec((1, 2, tq, nk), bias_map),
        ],
        out_specs=pl.BlockSpec((1, tq, LANES), lambda bi, hp, blk: (bi, blk, hp)),
        out_shape=jax.ShapeDtypeStruct((b, s, BRANCH_W), F32),
        compiler_params=_cparams("arbitrary", "arbitrary", "arbitrary"),
        name="na_lat",
    )(lat16, lat16, lat16, ctx16, ctx16, bias)


def _na_ctx_kernel(q_ref, k_ref, v_ref, o_ref):
    _na_head_pair(q_ref[0], [(k_ref[0], v_ref[0], None)], o_ref)


def _na_ctx(ctx16):
    b, sc, _ = ctx16.shape
    return pl.pallas_call(
        _na_ctx_kernel,
        grid=(b, NA_HEADS // 2),
        in_specs=[
            pl.BlockSpec((1, sc, LANES), lambda bi, hp: (bi, 0, NA_Q_BLK + hp)),
            pl.BlockSpec((1, sc, LANES), lambda bi, hp: (bi, 0, NA_K_BLK + hp)),
            pl.BlockSpec((1, sc, LANES), lambda bi, hp: (bi, 0, NA_V_BLK + hp)),
        ],
        out_specs=pl.BlockSpec((1, sc, LANES), lambda bi, hp: (bi, 0, hp)),
        out_shape=jax.ShapeDtypeStruct((b, sc, BRANCH_W), F32),
        compiler_params=_cparams("arbitrary", "arbitrary"),
        name="na_ctx",
    )(ctx16, ctx16, ctx16)


def _na_bias(tab, rows):
    w = GRID_W
    lead = tab.shape[:-1]
    left = w - NA_KW
    g = jnp.pad(tab, [(0, 0)] * len(lead) + [(left, 2 * w - left - (2 * NA_KW - 1))])
    g = jnp.broadcast_to(g[..., None, :], lead + (w, 2 * w)).reshape(lead + (2 * w * w,))
    e = g[..., :w * (2 * w - 1)].reshape(lead + (w, 2 * w - 1))[..., w - 1:]
    col = np.arange(w)
    cstart = np.clip(col - NA_KW // 2, 0, w - NA_KW)
    col_ok = (col[None, :] >= cstart[:, None]) & (col[None, :] < cstart[:, None] + NA_KW)
    e = jnp.where(col_ok, e, MASK_VALUE)
    masked = jnp.full(lead[:-1] + (w, w), MASK_VALUE, F32)
    r_blk = NA_ROWS_PER_BLOCK
    cases = []
    for r0, kstart in ((0, 0), (r_blk, r_blk - NA_KH // 2), (rows - r_blk, rows - NA_KEY_ROWS)):
        q_rows = []
        for qi in range(r_blk):
            r = r0 + qi
            rs = min(max(r - NA_KH // 2, 0), rows - NA_KH)
            blocks = []
            for ki in range(NA_KEY_ROWS):
                kr = kstart + ki
                blocks.append(e[..., kr - r + NA_KH - 1, :, :] if rs <= kr < rs + NA_KH else masked)
            q_rows.append(jnp.concatenate(blocks, axis=-1))
        cases.append(jnp.concatenate(q_rows, axis=-2))
    return jnp.stack(cases, axis=1)


def _final_kernel(x_ref, mod_ref, yf_ref, yd_ref, yp_ref, yn_ref, wg_ref, bg_ref, wb_ref, wo_ref,
                  g_ref, b_ref, o_ref, *, alpha):
    x = x_ref[0]
    xm = _modulated_ln(x, mod_ref).astype(BF16)
    bw = BRANCH_W
    d = D_MODEL
    acc = None
    for i, y_ref in enumerate((yf_ref, yd_ref, yp_ref, yn_ref)):
        zg = _dot(xm, wg_ref[:, i * bw:(i + 1) * bw]) + bg_ref[:, i * bw:(i + 1) * bw]
        gated = (y_ref[0] * (zg * _sigmoid(zg))).astype(BF16)
        proj = _dot(gated, wb_ref[i])
        c0 = N_BRANCH * bw + i * d
        zm = _dot(xm, wg_ref[:, c0:c0 + d]) + bg_ref[:, c0:c0 + d]
        term = _sigmoid(zm) * proj
        acc = term if acc is None else acc + term
    out = _dot(acc.astype(BF16), wo_ref[...])
    h = alpha * x + mod_ref[0, 2:3, :] * out
    mu = jnp.mean(h, axis=-1, keepdims=True)
    hc = h - mu
    var = jnp.mean(hc * hc, axis=-1, keepdims=True)
    o_ref[0] = hc * lax.rsqrt(var + LN_EPS) * g_ref[...] + b_ref[...]


def _final(x, mod, per_batch_mod, ys, wg, bg, wb, wo, ln_g, ln_b, alpha, tm):
    b, s, d = x.shape
    ng = wg.shape[1]
    mod_map = (lambda bi, i: (bi, 0, 0)) if per_batch_mod else (lambda bi, i: (0, 0, 0))
    tile = lambda w: pl.BlockSpec((1, tm, w), lambda bi, i: (bi, i, 0))
    const2 = lambda shape: pl.BlockSpec(shape, lambda bi, i: (0, 0), pipeline_mode=pl.Buffered(1))
    return pl.pallas_call(
        functools.partial(_final_kernel, alpha=alpha),
        grid=(b, s // tm),
        in_specs=[
            tile(d),
            pl.BlockSpec((1, 3, d), mod_map),
            tile(BRANCH_W), tile(BRANCH_W), tile(BRANCH_W), tile(BRANCH_W),
            const2((d, ng)),
            const2((1, ng)),
            pl.BlockSpec((N_BRANCH, BRANCH_W, d), lambda bi, i: (0, 0, 0), pipeline_mode=pl.Buffered(1)),
            const2((d, d)),
            const2((1, d)),
            const2((1, d)),
        ],
        out_specs=tile(d),
        out_shape=jax.ShapeDtypeStruct((b, s, d), F32),
        compiler_params=_cparams("arbitrary", "arbitrary"),
        name="gate_merge_out",
    )(x, mod, *ys, wg, bg, wb, wo, ln_g, ln_b)


def _rope_tables(s):
    t = jnp.arange(s)
    rows = (t // GRID_W).astype(F32)
    cols = (t % GRID_W).astype(F32)
    nf = DIFF_HD // 4
    inv = ROPE_BASE ** (-jnp.arange(nf, dtype=F32) / nf)
    ang = jnp.stack([rows[:, None] * inv, cols[:, None] * inv], axis=1)
    cos = jnp.cos(ang)
    sin = jnp.sin(ang)
    cos64 = jnp.concatenate([cos[:, 0], cos[:, 0], cos[:, 1], cos[:, 1]], axis=-1)
    sin64 = jnp.concatenate([-sin[:, 0], sin[:, 0], -sin[:, 1], sin[:, 1]], axis=-1)
    rep = LANES // DIFF_HD
    return jnp.tile(cos64, (1, rep)), jnp.tile(sin64, (1, rep))


def _dft_tables(n):
    idx = jnp.arange(n, dtype=jnp.int32)
    m = (idx[:, None] * idx[None, :]) % n
    ang = m.astype(F32) * (2.0 * math.pi / n)
    return jnp.cos(ang), jnp.sin(ang)


def _split_cols(w, sizes):
    out, acc = [], 0
    for sz in sizes:
        out.append(w[..., acc:acc + sz])
        acc += sz
    return out


def kernel(x, c, ctx, c_ctx, w_mod, b_mod, w_in, b_in, fnet_w, diff_lam, diff_subln, pool_w, pool_scale,
           na_bias, w_branch, w_out, ln_g, ln_b):
    bsz, s, d = x.shape
    s_ctx = ctx.shape[1]
    rows = s // GRID_W
    assert d == D_MODEL and s % (GRID_W * NA_ROWS_PER_BLOCK) == 0 and rows >= NA_KEY_ROWS
    alpha = (2.0 * DEPTH) ** 0.25

    z = _split_cols(w_in, (BRANCH_W,) * 12 + (N_BRANCH * D_MODEL,))
    zb = _split_cols(b_in, (BRANCH_W,) * 12 + (N_BRANCH * D_MODEL,))
    order_a = (0, 6, 2, 3, 4, 8, 9, 10)
    order_g = (1, 5, 7, 11, 12)
    wa = jnp.concatenate([z[i] for i in order_a], axis=-1).astype(BF16)
    ba = jnp.concatenate([zb[i] for i in order_a], axis=-1)
    wg = jnp.concatenate([z[i] for i in order_g], axis=-1).astype(BF16)
    bg = jnp.concatenate([zb[i] for i in order_g], axis=-1)
    wb16 = w_branch.astype(BF16)
    wo16 = w_out.astype(BF16)
    fw16 = fnet_w.astype(BF16)
    pw16 = pool_w.astype(BF16)

    rope_tabs = _rope_tables(s)
    cs_tabs = {}
    for n in {s, s_ctx}:
        cn, sn = _dft_tables(n)
        cs_tabs[n] = jnp.concatenate([cn, -sn], axis=1).astype(BF16)
    cc, sc_ = _dft_tables(FNET_GW)
    ccsc = jnp.concatenate([cc, sc_], axis=1).astype(BF16)
    na_bias_all = _na_bias(na_bias, rows)

    r_pad = -(-(bsz + 1) // 8) * 8
    cc_all = jnp.zeros((r_pad, d), F32).at[:bsz].set(c).at[bsz].set(c_ctx)
    mod_all = _modulation(cc_all, w_mod, b_mod)

    tm_lat = min(512, s)
    tm_fin = min(256, s)
    tq_diff = min(256, s)
    tn_fnet = min(512, s)

    h, hc = x, ctx
    for l in range(DEPTH):
        lambda_init = 0.8 - 0.6 * math.exp(-0.3 * l)
        need_ctx = l < DEPTH - 1
        mod_lat = mod_all[l, :bsz].reshape(bsz, 3, d)
        mod_ctx = mod_all[l, bsz:bsz + 1].reshape(1, 3, d)

        lat32, lat16 = _inproj(h, mod_lat, True, wa[l], ba[l][None], rope_tabs, tm_lat)
        ctx32, ctx16 = _inproj(hc, mod_ctx, False, wa[l], ba[l][None], None, s_ctx)

        lam = diff_lam[l]
        sub = diff_subln[l][None]
        psc = pool_scale[l][None]
        y_f = _fnet(lat32, cs_tabs[s], ccsc, fw16[l], tn_fnet)
        y_d = _diff_attn(lat16, [ctx16, lat16], lam, sub, lambda_init, tq_diff)
        y_p = _pool(lat32, pw16[l], psc)
        y_n = _na_lat(lat16, ctx16, na_bias_all[l])
        h_new = _final(h, mod_lat, True, (y_f, y_d, y_p, y_n), wg[l], bg[l][None], wb16[l], wo16[l],
                       ln_g[l][None], ln_b[l][None], alpha, tm_fin)
        if need_ctx:
            y_fc = _fnet(ctx32, cs_tabs[s_ctx], ccsc, fw16[l], s_ctx)
            y_dc = _diff_attn(ctx16, [ctx16], lam, sub, lambda_init, s_ctx)
            y_pc = _pool(ctx32, pw16[l], psc)
            y_nc = _na_ctx(ctx16)
            hc = _final(hc, mod_ctx, False, (y_fc, y_dc, y_pc, y_nc), wg[l], bg[l][None], wb16[l], wo16[l],
                        ln_g[l][None], ln_b[l][None], alpha, s_ctx)
        h = h_new
    return h
```

```python
import functools
import math

import numpy as np
import jax
import jax.numpy as jnp
from jax import lax
from jax.experimental import pallas as pl
from jax.experimental.pallas import tpu as pltpu

D_MODEL = 1024
DEPTH = 4
GRID_W = 64
N_BRANCH = 4
BRANCH_W = 512
FNET_GROUPS = 4
FNET_GW = BRANCH_W // FNET_GROUPS
DIFF_HEADS = 4
DIFF_HD = 64
DIFF_VD = 2 * DIFF_HD
POOL_WINDOWS = (2, 4, 8, 16)
POOL_GW = BRANCH_W // len(POOL_WINDOWS)
NA_HEADS = 8
NA_HD = BRANCH_W // NA_HEADS
NA_KH = 8
NA_KW = 16
ROPE_BASE = 10000.0
LN_EPS = 1e-6
SUBLN_EPS = 1e-5

LANES = 128
POOL_PAD = 16
NA_ROWS_PER_BLOCK = 4
NA_KEY_ROWS = NA_ROWS_PER_BLOCK + NA_KH - 1
MASK_VALUE = -1e30
VMEM_LIMIT = 56 * 1024 * 1024

F32 = jnp.float32
BF16 = jnp.bfloat16


def _cparams(*sem):
    return pltpu.CompilerParams(dimension_semantics=sem, vmem_limit_bytes=VMEM_LIMIT)


def _dot(a, b):
    return jnp.dot(a, b, preferred_element_type=F32)


def _dot_nt(a, b):
    return lax.dot_general(a, b, (((1,), (1,)), ((), ())), preferred_element_type=F32)


def _sigmoid(x):
    return 1.0 / (1.0 + jnp.exp(-x))


def _modulated_ln(x, mod_ref):
    mu = jnp.mean(x, axis=-1, keepdims=True)
    xc = x - mu
    var = jnp.mean(xc * xc, axis=-1, keepdims=True)
    y = xc * lax.rsqrt(var + LN_EPS)
    return y * (1.0 + mod_ref[0, 1:2, :]) + mod_ref[0, 0:1, :]


def _mod_kernel(c_ref, w_ref, b_ref, o_ref):
    c = c_ref[...]
    s = (c * _sigmoid(c)).astype(BF16)
    o_ref[0] = _dot(s, w_ref[0].astype(BF16)) + b_ref[0]


def _modulation(cc, w_mod, b_mod):
    n_l, d, d3 = w_mod.shape
    r = cc.shape[0]
    tn = d
    return pl.pallas_call(
        _mod_kernel,
        grid=(n_l, d3 // tn),
        in_specs=[
            pl.BlockSpec((r, d), lambda l, j: (0, 0)),
            pl.BlockSpec((1, d, tn), lambda l, j: (l, 0, j)),
            pl.BlockSpec((1, 1, tn), lambda l, j: (l, 0, j)),
        ],
        out_specs=pl.BlockSpec((1, r, tn), lambda l, j: (l, 0, j)),
        out_shape=jax.ShapeDtypeStruct((n_l, r, d3), F32),
        compiler_params=_cparams("arbitrary", "arbitrary"),
        name="modulation",
    )(cc, w_mod, b_mod.reshape(n_l, 1, d3))


def _rope(r, cos_ref, sin_ref):
    lane = lax.broadcasted_iota(jnp.int32, (r.shape[0], LANES), 1)
    first = (lane % 32) < 16
    cos = cos_ref[...]
    sin = sin_ref[...]
    outs = []
    for k in range(r.shape[1] // LANES):
        xk = r[:, k * LANES:(k + 1) * LANES]
        partner = jnp.where(first, pltpu.roll(xk, LANES - 16, 1), pltpu.roll(xk, 16, 1))
        outs.append(xk * cos + partner * sin)
    return jnp.concatenate(outs, axis=1)


def _inproj_kernel(*refs, rope):
    if rope:
        x_ref, mod_ref, w_ref, b_ref, cos_ref, sin_ref, o32_ref, o16_ref = refs
    else:
        x_ref, mod_ref, w_ref, b_ref, o32_ref, o16_ref = refs
    xm = _modulated_ln(x_ref[0], mod_ref).astype(BF16)
    bw = BRANCH_W
    for j in range(2):
        o32_ref[0, :, j * bw:(j + 1) * bw] = _dot(xm, w_ref[:, j * bw:(j + 1) * bw]) + b_ref[:, j * bw:(j + 1) * bw]
    for j in range(6):
        c0 = (2 + j) * bw
        r = _dot(xm, w_ref[:, c0:c0 + bw]) + b_ref[:, c0:c0 + bw]
        if rope and j in (0, 1):
            r = _rope(r, cos_ref, sin_ref)
        o16_ref[0, :, j * bw:(j + 1) * bw] = r.astype(BF16)


def _inproj(x, mod, per_batch_mod, wa, ba, rope_tabs, tm):
    b, s, d = x.shape
    na = wa.shape[1]
    mod_map = (lambda bi, i: (bi, 0, 0)) if per_batch_mod else (lambda bi, i: (0, 0, 0))
    in_specs = [
        pl.BlockSpec((1, tm, d), lambda bi, i: (bi, i, 0)),
        pl.BlockSpec((1, 3, d), mod_map),
        pl.BlockSpec((d, na), lambda bi, i: (0, 0)),
        pl.BlockSpec((1, na), lambda bi, i: (0, 0)),
    ]
    args = [x, mod, wa, ba]
    if rope_tabs is not None:
        in_specs += [pl.BlockSpec((tm, LANES), lambda bi, i: (i, 0))] * 2
        args += list(rope_tabs)
    return pl.pallas_call(
        functools.partial(_inproj_kernel, rope=rope_tabs is not None),
        grid=(b, s // tm),
        in_specs=in_specs,
        out_specs=[
            pl.BlockSpec((1, tm, 2 * BRANCH_W), lambda bi, i: (bi, i, 0)),
            pl.BlockSpec((1, tm, 6 * BRANCH_W), lambda bi, i: (bi, i, 0)),
        ],
        out_shape=[
            jax.ShapeDtypeStruct((b, s, 2 * BRANCH_W), F32),
            jax.ShapeDtypeStruct((b, s, 6 * BRANCH_W), BF16),
        ],
        compiler_params=_cparams("arbitrary", "arbitrary"),
        name="inproj_rope" if rope_tabs is not None else "inproj",
    )(*args)


def _fnet_kernel(u_ref, cs_ref, cc_ref, w_ref, o_ref, v_ref, *, scale):
    s = u_ref.shape[1]
    gw = FNET_GW

    @pl.when(pl.program_id(1) == 0)
    def _():
        for g in range(FNET_GROUPS):
            ug = u_ref[0, :, g * gw:(g + 1) * gw].astype(BF16)
            t = _dot(ug, cc_ref[...])
            v_ref[0:s, g * gw:(g + 1) * gw] = t[:, :gw].astype(BF16)
            v_ref[s:2 * s, g * gw:(g + 1) * gw] = t[:, gw:].astype(BF16)

    f = _dot(cs_ref[...], v_ref[...]) * scale
    for g in range(FNET_GROUPS):
        o_ref[0, :, g * gw:(g + 1) * gw] = _dot(f[:, g * gw:(g + 1) * gw].astype(BF16), w_ref[g])


def _fnet(o32, cs, ccsc, w, tn):
    b, s, _ = o32.shape
    scale = 1.0 / math.sqrt(s * FNET_GW)
    return pl.pallas_call(
        functools.partial(_fnet_kernel, scale=scale),
        grid=(b, s // tn),
        in_specs=[
            pl.BlockSpec((1, s, BRANCH_W), lambda bi, i: (bi, 0, 0)),
            pl.BlockSpec((tn, 2 * s), lambda bi, i: (i, 0)),
            pl.BlockSpec((FNET_GW, 2 * FNET_GW), lambda bi, i: (0, 0)),
            pl.BlockSpec((FNET_GROUPS, FNET_GW, FNET_GW), lambda bi, i: (0, 0, 0)),
        ],
        out_specs=pl.BlockSpec((1, tn, BRANCH_W), lambda bi, i: (bi, i, 0)),
        out_shape=jax.ShapeDtypeStruct((b, s, BRANCH_W), F32),
        scratch_shapes=[pltpu.VMEM((2 * s, BRANCH_W), BF16)],
        compiler_params=_cparams("arbitrary", "arbitrary"),
        name="fnet",
    )(o32, cs, ccsc, w)


def _pool_kernel(u_ref, w_ref, sc_ref, o_ref, pad_ref):
    s = u_ref.shape[1]
    gw = POOL_GW
    t = lax.broadcasted_iota(jnp.int32, (s, 1), 0)
    zeros = jnp.zeros((POOL_PAD, gw), F32)
    for g, win in enumerate(POOL_WINDOWS):
        u = u_ref[0, :, g * gw:(g + 1) * gw]
        pad_ref[0:POOL_PAD, :] = zeros
        pad_ref[POOL_PAD + s:2 * POOL_PAD + s, :] = zeros
        pad_ref[POOL_PAD:POOL_PAD + s, :] = u
        acc = None
        for j in range(-(win // 2), win - win // 2):
            term = pad_ref[POOL_PAD + j:POOL_PAD + j + s, :]
            acc = term if acc is None else acc + term
        lo = jnp.clip(t - win // 2, 0, s)
        hi = jnp.clip(t - win // 2 + win, 0, s)
        cnt = (hi - lo).astype(F32)
        pooled = (acc / cnt - u).astype(BF16)
        y = _dot(pooled, w_ref[g])
        o_ref[0, :, g * gw:(g + 1) * gw] = y * sc_ref[:, g * gw:(g + 1) * gw]


def _pool(o32, w, scale):
    b, s, _ = o32.shape
    return pl.pallas_call(
        _pool_kernel,
        grid=(b,),
        in_specs=[
            pl.BlockSpec((1, s, BRANCH_W), lambda bi: (bi, 0, 1)),
            pl.BlockSpec((len(POOL_WINDOWS), POOL_GW, POOL_GW), lambda bi: (0, 0, 0)),
            pl.BlockSpec((1, BRANCH_W), lambda bi: (0, 0)),
        ],
        out_specs=pl.BlockSpec((1, s, BRANCH_W), lambda bi: (bi, 0, 0)),
        out_shape=jax.ShapeDtypeStruct((b, s, BRANCH_W), F32),
        scratch_shapes=[pltpu.VMEM((s + 2 * POOL_PAD, POOL_GW), F32)],
        compiler_params=_cparams("arbitrary"),
        name="pool",
    )(o32, w, scale)


def _softmax_parts(scores):
    m = functools.reduce(jnp.maximum, [jnp.max(s, axis=-1, keepdims=True) for s in scores])
    es = [jnp.exp(s - m) for s in scores]
    den = functools.reduce(lambda a, c: a + c, [jnp.sum(e, axis=-1, keepdims=True) for e in es])
    return es, den


def _diff_kernel(*refs, n_seg, lambda_init):
    q_ref = refs[0]
    k_refs = refs[1:1 + n_seg]
    v_refs = refs[1 + n_seg:1 + 2 * n_seg]
    lam_ref, sub_ref, o_ref = refs[1 + 2 * n_seg:]
    lv = lam_ref[...]
    lam = (jnp.exp(jnp.sum(lv[0:1] * lv[1:2], axis=-1, keepdims=True))
           - jnp.exp(jnp.sum(lv[2:3] * lv[3:4], axis=-1, keepdims=True)) + lambda_init)
    tq = q_ref.shape[1]
    lane = lax.broadcasted_iota(jnp.int32, (tq, LANES), 1)
    zero = jnp.zeros((tq, LANES), BF16)
    q = q_ref[0] * (DIFF_HD ** -0.5)
    qs = jnp.concatenate([jnp.where(lane < DIFF_HD, q, zero), jnp.where(lane >= DIFF_HD, q, zero)], axis=0)
    es, den = _softmax_parts([_dot_nt(qs, k_ref[0]) for k_ref in k_refs])
    l1 = den[:tq]
    rho = lam * l1 / den[tq:]
    o = None
    for seg in range(n_seg):
        a = (es[seg][:tq] - rho * es[seg][tq:]).astype(BF16)
        pv = _dot(a, v_refs[seg][0])
        o = pv if o is None else o + pv
    o = o / l1
    o = o * lax.rsqrt(jnp.mean(o * o, axis=-1, keepdims=True) + SUBLN_EPS)
    o_ref[0] = o * sub_ref[...] * (1.0 - lambda_init)


def _diff_attn(q16, kv16_list, lam_vecs, subln, lambda_init, tq):
    b, sq, _ = q16.shape
    n_seg = len(kv16_list)
    nh = DIFF_HEADS
    in_specs = [pl.BlockSpec((1, tq, LANES), lambda bi, h, i: (bi, i, h))]
    in_specs += [pl.BlockSpec((1, kv.shape[1], LANES), lambda bi, h, i: (bi, 0, nh + h)) for kv in kv16_list]
    in_specs += [pl.BlockSpec((1, kv.shape[1], LANES), lambda bi, h, i: (bi, 0, 2 * nh + h)) for kv in kv16_list]
    in_specs += [
        pl.BlockSpec((4, DIFF_HD), lambda bi, h, i: (0, 0)),
        pl.BlockSpec((1, DIFF_VD), lambda bi, h, i: (0, 0)),
    ]
    return pl.pallas_call(
        functools.partial(_diff_kernel, n_seg=n_seg, lambda_init=lambda_init),
        grid=(b, nh, sq // tq),
        in_specs=in_specs,
        out_specs=pl.BlockSpec((1, tq, LANES), lambda bi, h, i: (bi, i, h)),
        out_shape=jax.ShapeDtypeStruct((b, sq, BRANCH_W), F32),
        compiler_params=_cparams("arbitrary", "arbitrary", "arbitrary"),
        name="diff_attn",
    )(q16, *kv16_list, *kv16_list, lam_vecs, subln)


DIFF_KEY_CHUNK = 512
DIFF_ROW_BLOCK = 64


def _key_chunks(seg_lens):
    chunks, col = [], 0
    for seg, n in enumerate(seg_lens):
        for r0 in range(0, n, DIFF_KEY_CHUNK):
            w = min(DIFF_KEY_CHUNK, n - r0)
            chunks.append((seg, r0, col, w))
            col += w
    return chunks


def _diff_pipe_kernel(q_ref, kc_ref, kl_ref, vc_ref, vl_ref, lam_ref, sub_ref, o_ref,
                      s_scr, e_scr, m_scr, l_scr, ma_scr, la_scr, *, lambda_init):
    g = pl.program_id(0)
    tq = q_ref.shape[1]
    k_refs = (kc_ref, kl_ref)
    v_refs = (vc_ref, vl_ref)
    chunks = _key_chunks([r.shape[1] for r in k_refs])

    @pl.when(g == 0)
    def _():
        s_scr[...] = jnp.zeros(s_scr.shape, F32)
        e_scr[...] = jnp.zeros(e_scr.shape, BF16)
        m_scr[...] = jnp.zeros(m_scr.shape, F32)
        l_scr[...] = jnp.ones(l_scr.shape, F32)

    lane = lax.broadcasted_iota(jnp.int32, (tq, LANES), 1)
    zero = jnp.zeros((tq, LANES), BF16)
    q = q_ref[0] * (DIFF_HD ** -0.5)
    qs = jnp.concatenate([jnp.where(lane < DIFF_HD, q, zero), jnp.where(lane >= DIFF_HD, q, zero)], axis=0)
    l_fin = l_scr[...]
    acc = None
    n_rb = 2 * tq // DIFF_ROW_BLOCK
    for ci, (seg, r0, c0, w) in enumerate(chunks):
        first, last = ci == 0, ci == len(chunks) - 1
        pv = _dot(e_scr[:, c0:c0 + w], v_refs[seg][0, r0:r0 + w, :])
        acc = pv if acc is None else acc + pv
        for rb in range(n_rb):
            rows = slice(rb * DIFF_ROW_BLOCK, (rb + 1) * DIFF_ROW_BLOCK)
            m_prev = m_scr[rows, :]
            l_run = None if first else la_scr[rows, :]
            for j in range(w // LANES):
                cols = slice(c0 + j * LANES, c0 + (j + 1) * LANES)
                ej = jnp.exp(s_scr[rows, cols] - m_prev)
                l_run = ej if l_run is None else l_run + ej
                e_scr[rows, cols] = ej.astype(BF16)
            if last:
                l_scr[rows, :] = jnp.broadcast_to(jnp.sum(l_run, axis=1, keepdims=True), l_run.shape)
            else:
                la_scr[rows, :] = l_run
        s_scr[:, c0:c0 + w] = _dot_nt(qs, k_refs[seg][0, r0:r0 + w, :])
        for rb in range(n_rb):
            rows = slice(rb * DIFF_ROW_BLOCK, (rb + 1) * DIFF_ROW_BLOCK)
            m_run = None if first else ma_scr[rows, :]
            for j in range(w // LANES):
                sj = s_scr[rows, c0 + j * LANES:c0 + (j + 1) * LANES]
                m_run = sj if m_run is None else jnp.maximum(m_run, sj)
            if last:
                m_scr[rows, :] = jnp.broadcast_to(jnp.max(m_run, axis=1, keepdims=True), m_run.shape)
            else:
                ma_scr[rows, :] = m_run

    lv = lam_ref[...]
    lam = (jnp.exp(jnp.sum(lv[0:1] * lv[1:2], axis=-1, keepdims=True))
           - jnp.exp(jnp.sum(lv[2:3] * lv[3:4], axis=-1, keepdims=True)) + lambda_init)
    o = acc / l_fin
    o = o[:tq] - lam * o[tq:]
    o = o * lax.rsqrt(jnp.mean(o * o, axis=-1, keepdims=True) + SUBLN_EPS)
    o_ref[0] = o * sub_ref[...] * (1.0 - lambda_init)


def _diff_attn_lat(lat16, ctx16, lam_vecs, subln, lambda_init, tq):
    b, s, _ = lat16.shape
    sc = ctx16.shape[1]
    nh = DIFF_HEADS
    nt = s // tq
    n_tiles = b * nh * nt

    def tile(g, lag):
        t = jnp.clip(g - lag, 0, n_tiles - 1)
        return t // (nh * nt), (t // nt) % nh, t % nt

    def q_map(g):
        bi, h, i = tile(g, 0)
        return (bi, i, h)

    def k_map(g):
        bi, h, _ = tile(g, 0)
        return (bi, 0, nh + h)

    def v_map(g):
        bi, h, _ = tile(g, 2)
        return (bi, 0, 2 * nh + h)

    def o_map(g):
        bi, h, i = tile(g, 2)
        return (bi, i, h)

    nk = sc + s
    stat = pltpu.VMEM((2 * tq, LANES), F32)
    return pl.pallas_call(
        functools.partial(_diff_pipe_kernel, lambda_init=lambda_init),
        grid=(n_tiles + 2,),
        in_specs=[
            pl.BlockSpec((1, tq, LANES), q_map),
            pl.BlockSpec((1, sc, LANES), k_map),
            pl.BlockSpec((1, s, LANES), k_map),
            pl.BlockSpec((1, sc, LANES), v_map),
            pl.BlockSpec((1, s, LANES), v_map),
            pl.BlockSpec((4, DIFF_HD), lambda g: (0, 0)),
            pl.BlockSpec((1, DIFF_VD), lambda g: (0, 0)),
        ],
        out_specs=pl.BlockSpec((1, tq, LANES), o_map),
        out_shape=jax.ShapeDtypeStruct((b, s, BRANCH_W), F32),
        scratch_shapes=[pltpu.VMEM((2 * tq, nk), F32), pltpu.VMEM((2 * tq, nk), BF16), stat, stat, stat, stat],
        compiler_params=_cparams("arbitrary"),
        name="diff_attn_lat",
    )(lat16, ctx16, lat16, ctx16, lat16, lam_vecs, subln)


NA_Q_BLK, NA_K_BLK, NA_V_BLK = 3, 4, 5


def _na_heads(q_ref, segs, o_ref):
    tq = q_ref.shape[1]
    lane = lax.broadcasted_iota(jnp.int32, (tq, LANES), 1)
    zero = jnp.zeros((tq, LANES), BF16)
    for hp in range(NA_HEADS // 2):
        q = q_ref[0, :, hp * LANES:(hp + 1) * LANES] * (NA_HD ** -0.5)
        qs = jnp.concatenate([jnp.where(lane < NA_HD, q, zero), jnp.where(lane >= NA_HD, q, zero)], axis=0)
        scores = []
        for k_of, _, bias_of in segs:
            sc = _dot_nt(qs, k_of(hp))
            if bias_of is not None:
                sc = sc + bias_of(hp)
            scores.append(sc)
        es, den = _softmax_parts(scores)
        o = None
        for (_, v_of, _), e in zip(segs, es):
            pv = _dot(e.astype(BF16), v_of(hp))
            o = pv if o is None else o + pv
        o = o / den
        o_ref[0, :, hp * LANES:(hp + 1) * LANES] = jnp.where(lane < NA_HD, o[:tq], o[tq:])


def _na_lat_kernel(q_ref, k_ref, v_ref, kc_ref, vc_ref, bias_ref, o_ref, *, rows):
    blk = pl.program_id(1)
    kstart = jnp.clip(blk * NA_ROWS_PER_BLOCK - NA_KH // 2, 0, rows - NA_KEY_ROWS)
    off = pl.multiple_of(kstart * GRID_W, GRID_W)
    nk = NA_KEY_ROWS * GRID_W
    pair = lambda hp: slice(hp * LANES, (hp + 1) * LANES)
    window = (lambda hp: k_ref[0, pl.ds(off, nk), pair(hp)],
              lambda hp: v_ref[0, pl.ds(off, nk), pair(hp)],
              lambda hp: jnp.concatenate([bias_ref[0, 2 * hp], bias_ref[0, 2 * hp + 1]], axis=0))
    context = (lambda hp: kc_ref[0, :, pair(hp)], lambda hp: vc_ref[0, :, pair(hp)], None)
    _na_heads(q_ref, [window, context], o_ref)


def _na_lat(lat16, ctx16, bias):
    b, s, _ = lat16.shape
    sc = ctx16.shape[1]
    rows = s // GRID_W
    nblk = rows // NA_ROWS_PER_BLOCK
    tq = NA_ROWS_PER_BLOCK * GRID_W
    nk = NA_KEY_ROWS * GRID_W
    bw = BRANCH_W

    def bias_map(bi, blk):
        case = jnp.where(blk == 0, 0, jnp.where(blk == nblk - 1, 2, 1))
        return (case, 0, 0, 0)

    return pl.pallas_call(
        functools.partial(_na_lat_kernel, rows=rows),
        grid=(b, nblk),
        in_specs=[
            pl.BlockSpec((1, tq, bw), lambda bi, blk: (bi, blk, NA_Q_BLK)),
            pl.BlockSpec((1, s, bw), lambda bi, blk: (bi, 0, NA_K_BLK)),
            pl.BlockSpec((1, s, bw), lambda bi, blk: (bi, 0, NA_V_BLK)),
            pl.BlockSpec((1, sc, bw), lambda bi, blk: (bi, 0, NA_K_BLK)),
            pl.BlockSpec((1, sc, bw), lambda bi, blk: (bi, 0, NA_V_BLK)),
            pl.BlockSpec((1, NA_HEADS, tq, nk), bias_map),
        ],
        out_specs=pl.BlockSpec((1, tq, bw), lambda bi, blk: (bi, blk, 0)),
        out_shape=jax.ShapeDtypeStruct((b, s, bw), F32),
        compiler_params=_cparams("arbitrary", "arbitrary"),
        name="na_lat",
    )(lat16, lat16, lat16, ctx16, ctx16, bias)


def _na_ctx_kernel(q_ref, k_ref, v_ref, o_ref):
    pair = lambda hp: slice(hp * LANES, (hp + 1) * LANES)
    _na_heads(q_ref, [(lambda hp: k_ref[0, :, pair(hp)], lambda hp: v_ref[0, :, pair(hp)], None)], o_ref)


def _na_ctx(ctx16):
    b, sc, _ = ctx16.shape
    bw = BRANCH_W
    return pl.pallas_call(
        _na_ctx_kernel,
        grid=(b,),
        in_specs=[
            pl.BlockSpec((1, sc, bw), lambda bi: (bi, 0, NA_Q_BLK)),
            pl.BlockSpec((1, sc, bw), lambda bi: (bi, 0, NA_K_BLK)),
            pl.BlockSpec((1, sc, bw), lambda bi: (bi, 0, NA_V_BLK)),
        ],
        out_specs=pl.BlockSpec((1, sc, bw), lambda bi: (bi, 0, 0)),
        out_shape=jax.ShapeDtypeStruct((b, sc, bw), F32),
        compiler_params=_cparams("arbitrary"),
        name="na_ctx",
    )(ctx16, ctx16, ctx16)


def _na_bias(tab, rows):
    w = GRID_W
    lead = tab.shape[:-1]
    left = w - NA_KW
    g = jnp.pad(tab, [(0, 0)] * len(lead) + [(left, 2 * w - left - (2 * NA_KW - 1))])
    g = jnp.broadcast_to(g[..., None, :], lead + (w, 2 * w)).reshape(lead + (2 * w * w,))
    e = g[..., :w * (2 * w - 1)].reshape(lead + (w, 2 * w - 1))[..., w - 1:]
    col = np.arange(w)
    cstart = np.clip(col - NA_KW // 2, 0, w - NA_KW)
    col_ok = (col[None, :] >= cstart[:, None]) & (col[None, :] < cstart[:, None] + NA_KW)
    e = jnp.where(col_ok, e, MASK_VALUE)
    masked = jnp.full(lead[:-1] + (w, w), MASK_VALUE, F32)
    r_blk = NA_ROWS_PER_BLOCK
    cases = []
    for r0, kstart in ((0, 0), (r_blk, r_blk - NA_KH // 2), (rows - r_blk, rows - NA_KEY_ROWS)):
        q_rows = []
        for qi in range(r_blk):
            r = r0 + qi
            rs = min(max(r - NA_KH // 2, 0), rows - NA_KH)
            blocks = []
            for ki in range(NA_KEY_ROWS):
                kr = kstart + ki
                blocks.append(e[..., kr - r + NA_KH - 1, :, :] if rs <= kr < rs + NA_KH else masked)
            q_rows.append(jnp.concatenate(blocks, axis=-1))
        cases.append(jnp.concatenate(q_rows, axis=-2))
    return jnp.stack(cases, axis=1)


def _final_kernel(x_ref, mod_ref, yf_ref, yd_ref, yp_ref, yn_ref, wg_ref, bg_ref, wb_ref, wo_ref,
                  g_ref, b_ref, o_ref, *, alpha):
    x = x_ref[0]
    xm = _modulated_ln(x, mod_ref).astype(BF16)
    bw = BRANCH_W
    d = D_MODEL
    acc = None
    for i, y_ref in enumerate((yf_ref, yd_ref, yp_ref, yn_ref)):
        zg = _dot(xm, wg_ref[:, i * bw:(i + 1) * bw]) + bg_ref[:, i * bw:(i + 1) * bw]
        gated = (y_ref[0] * (zg * _sigmoid(zg))).astype(BF16)
        proj = _dot(gated, wb_ref[i])
        c0 = N_BRANCH * bw + i * d
        zm = _dot(xm, wg_ref[:, c0:c0 + d]) + bg_ref[:, c0:c0 + d]
        term = _sigmoid(zm) * proj
        acc = term if acc is None else acc + term
    out = _dot(acc.astype(BF16), wo_ref[...])
    h = alpha * x + mod_ref[0, 2:3, :] * out
    mu = jnp.mean(h, axis=-1, keepdims=True)
    hc = h - mu
    var = jnp.mean(hc * hc, axis=-1, keepdims=True)
    o_ref[0] = hc * lax.rsqrt(var + LN_EPS) * g_ref[...] + b_ref[...]


def _final(x, mod, per_batch_mod, ys, wg, bg, wb, wo, ln_g, ln_b, alpha, tm):
    b, s, d = x.shape
    ng = wg.shape[1]
    mod_map = (lambda bi, i: (bi, 0, 0)) if per_batch_mod else (lambda bi, i: (0, 0, 0))
    tile = lambda w: pl.BlockSpec((1, tm, w), lambda bi, i: (bi, i, 0))
    const2 = lambda shape: pl.BlockSpec(shape, lambda bi, i: (0, 0), pipeline_mode=pl.Buffered(1))
    return pl.pallas_call(
        functools.partial(_final_kernel, alpha=alpha),
        grid=(b, s // tm),
        in_specs=[
            tile(d),
            pl.BlockSpec((1, 3, d), mod_map),
            tile(BRANCH_W), tile(BRANCH_W), tile(BRANCH_W), tile(BRANCH_W),
            const2((d, ng)),
            const2((1, ng)),
            pl.BlockSpec((N_BRANCH, BRANCH_W, d), lambda bi, i: (0, 0, 0), pipeline_mode=pl.Buffered(1)),
            const2((d, d)),
            const2((1, d)),
            const2((1, d)),
        ],
        out_specs=tile(d),
        out_shape=jax.ShapeDtypeStruct((b, s, d), F32),
        compiler_params=_cparams("arbitrary", "arbitrary"),
        name="gate_merge_out",
    )(x, mod, *ys, wg, bg, wb, wo, ln_g, ln_b)


def _rope_tables(s):
    t = jnp.arange(s)
    rows = (t // GRID_W).astype(F32)
    cols = (t % GRID_W).astype(F32)
    nf = DIFF_HD // 4
    inv = ROPE_BASE ** (-jnp.arange(nf, dtype=F32) / nf)
    ang = jnp.stack([rows[:, None] * inv, cols[:, None] * inv], axis=1)
    cos = jnp.cos(ang)
    sin = jnp.sin(ang)
    cos64 = jnp.concatenate([cos[:, 0], cos[:, 0], cos[:, 1], cos[:, 1]], axis=-1)
    sin64 = jnp.concatenate([-sin[:, 0], sin[:, 0], -sin[:, 1], sin[:, 1]], axis=-1)
    rep = LANES // DIFF_HD
    return jnp.tile(cos64, (1, rep)), jnp.tile(sin64, (1, rep))


def _dft_tables(n):
    idx = jnp.arange(n, dtype=jnp.int32)
    m = (idx[:, None] * idx[None, :]) % n
    ang = m.astype(F32) * (2.0 * math.pi / n)
    return jnp.cos(ang), jnp.sin(ang)


def _split_cols(w, sizes):
    out, acc = [], 0
    for sz in sizes:
        out.append(w[..., acc:acc + sz])
        acc += sz
    return out


def kernel(x, c, ctx, c_ctx, w_mod, b_mod, w_in, b_in, fnet_w, diff_lam, diff_subln, pool_w, pool_scale,
           na_bias, w_branch, w_out, ln_g, ln_b):
    bsz, s, d = x.shape
    s_ctx = ctx.shape[1]
    rows = s // GRID_W
    assert d == D_MODEL and s % (GRID_W * NA_ROWS_PER_BLOCK) == 0 and rows >= NA_KEY_ROWS
    alpha = (2.0 * DEPTH) ** 0.25

    z = _split_cols(w_in, (BRANCH_W,) * 12 + (N_BRANCH * D_MODEL,))
    zb = _split_cols(b_in, (BRANCH_W,) * 12 + (N_BRANCH * D_MODEL,))
    order_a = (0, 6, 2, 3, 4, 8, 9, 10)
    order_g = (1, 5, 7, 11, 12)
    wa = jnp.concatenate([z[i] for i in order_a], axis=-1).astype(BF16)
    ba = jnp.concatenate([zb[i] for i in order_a], axis=-1)
    wg = jnp.concatenate([z[i] for i in order_g], axis=-1).astype(BF16)
    bg = jnp.concatenate([zb[i] for i in order_g], axis=-1)
    wb16 = w_branch.astype(BF16)
    wo16 = w_out.astype(BF16)
    fw16 = fnet_w.astype(BF16)
    pw16 = pool_w.astype(BF16)

    rope_tabs = _rope_tables(s)
    cs_tabs = {}
    for n in {s, s_ctx}:
        cn, sn = _dft_tables(n)
        cs_tabs[n] = jnp.concatenate([cn, -sn], axis=1).astype(BF16)
    cc, sc_ = _dft_tables(FNET_GW)
    ccsc = jnp.concatenate([cc, sc_], axis=1).astype(BF16)
    na_bias_all = _na_bias(na_bias, rows)

    r_pad = -(-(bsz + 1) // 8) * 8
    cc_all = jnp.zeros((r_pad, d), F32).at[:bsz].set(c).at[bsz].set(c_ctx)
    mod_all = _modulation(cc_all, w_mod, b_mod)

    tm_lat = min(512, s)
    tm_fin = min(256, s)
    tq_diff = min(256, s)
    tn_fnet = min(512, s)

    h, hc = x, ctx
    for l in range(DEPTH):
        lambda_init = 0.8 - 0.6 * math.exp(-0.3 * l)
        need_ctx = l < DEPTH - 1
        mod_lat = mod_all[l, :bsz].reshape(bsz, 3, d)
        mod_ctx = mod_all[l, bsz:bsz + 1].reshape(1, 3, d)

        lat32, lat16 = _inproj(h, mod_lat, True, wa[l], ba[l][None], rope_tabs, tm_lat)
        ctx32, ctx16 = _inproj(hc, mod_ctx, False, wa[l], ba[l][None], None, s_ctx)

        lam = diff_lam[l]
        sub = diff_subln[l][None]
        psc = pool_scale[l][None]
        y_f = _fnet(lat32, cs_tabs[s], ccsc, fw16[l], tn_fnet)
        y_d = _diff_attn_lat(lat16, ctx16, lam, sub, lambda_init, tq_diff)
        y_p = _pool(lat32, pw16[l], psc)
        y_n = _na_lat(lat16, ctx16, na_bias_all[l])
        h_new = _final(h, mod_lat, True, (y_f, y_d, y_p, y_n), wg[l], bg[l][None], wb16[l], wo16[l],
                       ln_g[l][None], ln_b[l][None], alpha, tm_fin)
        if need_ctx:
            y_fc = _fnet(ctx32, cs_tabs[s_ctx], ccsc, fw16[l], s_ctx)
            y_dc = _diff_attn(ctx16, [ctx16], lam, sub, lambda_init, s_ctx)
            y_pc = _pool(ctx32, pw16[l], psc)
            y_nc = _na_ctx(ctx16)
            hc = _final(hc, mod_ctx, False, (y_fc, y_dc, y_pc, y_nc), wg[l], bg[l][None], wb16[l], wo16[l],
                        ln_g[l][None], ln_b[l][None], alpha, s_ctx)
        h = h_new
    return h
```

```python
import functools
import math

import numpy as np
import jax
import jax.numpy as jnp
from jax import lax
from jax.experimental import pallas as pl
from jax.experimental.pallas import tpu as pltpu

D_MODEL = 1024
DEPTH = 4
GRID_W = 64
N_BRANCH = 4
BRANCH_W = 512
FNET_GROUPS = 4
FNET_GW = BRANCH_W // FNET_GROUPS
DIFF_HEADS = 4
DIFF_HD = 64
DIFF_VD = 2 * DIFF_HD
POOL_WINDOWS = (2, 4, 8, 16)
POOL_GW = BRANCH_W // len(POOL_WINDOWS)
NA_HEADS = 8
NA_HD = BRANCH_W // NA_HEADS
NA_KH = 8
NA_KW = 16
ROPE_BASE = 10000.0
LN_EPS = 1e-6
SUBLN_EPS = 1e-5

LANES = 128
POOL_PAD = 16
NA_ROWS_PER_BLOCK = 4
NA_KEY_ROWS = NA_ROWS_PER_BLOCK + NA_KH - 1
MASK_VALUE = -1e30
VMEM_LIMIT = 56 * 1024 * 1024

F32 = jnp.float32
BF16 = jnp.bfloat16


def _cparams(*sem):
    return pltpu.CompilerParams(dimension_semantics=sem, vmem_limit_bytes=VMEM_LIMIT)


def _dot(a, b):
    return jnp.dot(a, b, preferred_element_type=F32)


def _dot_nt(a, b):
    return lax.dot_general(a, b, (((1,), (1,)), ((), ())), preferred_element_type=F32)


def _sigmoid(x):
    return 1.0 / (1.0 + jnp.exp(-x))


def _modulated_ln(x, mod_ref):
    mu = jnp.mean(x, axis=-1, keepdims=True)
    xc = x - mu
    var = jnp.mean(xc * xc, axis=-1, keepdims=True)
    y = xc * lax.rsqrt(var + LN_EPS)
    return y * (1.0 + mod_ref[0, 1:2, :]) + mod_ref[0, 0:1, :]


def _mod_kernel(c_ref, w_ref, b_ref, o_ref):
    c = c_ref[...]
    s = (c * _sigmoid(c)).astype(BF16)
    o_ref[0] = _dot(s, w_ref[0].astype(BF16)) + b_ref[0]


def _modulation(cc, w_mod, b_mod):
    n_l, d, d3 = w_mod.shape
    r = cc.shape[0]
    tn = d
    return pl.pallas_call(
        _mod_kernel,
        grid=(n_l, d3 // tn),
        in_specs=[
            pl.BlockSpec((r, d), lambda l, j: (0, 0)),
            pl.BlockSpec((1, d, tn), lambda l, j: (l, 0, j)),
            pl.BlockSpec((1, 1, tn), lambda l, j: (l, 0, j)),
        ],
        out_specs=pl.BlockSpec((1, r, tn), lambda l, j: (l, 0, j)),
        out_shape=jax.ShapeDtypeStruct((n_l, r, d3), F32),
        compiler_params=_cparams("arbitrary", "arbitrary"),
        name="modulation",
    )(cc, w_mod, b_mod.reshape(n_l, 1, d3))


def _rope(r, cos_ref, sin_ref):
    lane = lax.broadcasted_iota(jnp.int32, (r.shape[0], LANES), 1)
    first = (lane % 32) < 16
    cos = cos_ref[...]
    sin = sin_ref[...]
    outs = []
    for k in range(r.shape[1] // LANES):
        xk = r[:, k * LANES:(k + 1) * LANES]
        partner = jnp.where(first, pltpu.roll(xk, LANES - 16, 1), pltpu.roll(xk, 16, 1))
        outs.append(xk * cos + partner * sin)
    return jnp.concatenate(outs, axis=1)


def _inproj_kernel(*refs, rope):
    if rope:
        x_ref, mod_ref, w_ref, b_ref, cos_ref, sin_ref, o32_ref, o16_ref = refs
    else:
        x_ref, mod_ref, w_ref, b_ref, o32_ref, o16_ref = refs
    xm = _modulated_ln(x_ref[0], mod_ref).astype(BF16)
    bw = BRANCH_W
    for j in range(2):
        o32_ref[0, :, j * bw:(j + 1) * bw] = _dot(xm, w_ref[:, j * bw:(j + 1) * bw]) + b_ref[:, j * bw:(j + 1) * bw]
    for j in range(6):
        c0 = (2 + j) * bw
        r = _dot(xm, w_ref[:, c0:c0 + bw]) + b_ref[:, c0:c0 + bw]
        if rope and j in (0, 1):
            r = _rope(r, cos_ref, sin_ref)
        o16_ref[0, :, j * bw:(j + 1) * bw] = r.astype(BF16)


def _inproj(x, mod, per_batch_mod, wa, ba, rope_tabs, tm):
    b, s, d = x.shape
    na = wa.shape[1]
    mod_map = (lambda bi, i: (bi, 0, 0)) if per_batch_mod else (lambda bi, i: (0, 0, 0))
    in_specs = [
        pl.BlockSpec((1, tm, d), lambda bi, i: (bi, i, 0)),
        pl.BlockSpec((1, 3, d), mod_map),
        pl.BlockSpec((d, na), lambda bi, i: (0, 0)),
        pl.BlockSpec((1, na), lambda bi, i: (0, 0)),
    ]
    args = [x, mod, wa, ba]
    if rope_tabs is not None:
        in_specs += [pl.BlockSpec((tm, LANES), lambda bi, i: (i, 0))] * 2
        args += list(rope_tabs)
    return pl.pallas_call(
        functools.partial(_inproj_kernel, rope=rope_tabs is not None),
        grid=(b, s // tm),
        in_specs=in_specs,
        out_specs=[
            pl.BlockSpec((1, tm, 2 * BRANCH_W), lambda bi, i: (bi, i, 0)),
            pl.BlockSpec((1, tm, 6 * BRANCH_W), lambda bi, i: (bi, i, 0)),
        ],
        out_shape=[
            jax.ShapeDtypeStruct((b, s, 2 * BRANCH_W), F32),
            jax.ShapeDtypeStruct((b, s, 6 * BRANCH_W), BF16),
        ],
        compiler_params=_cparams("arbitrary", "arbitrary"),
        name="inproj_rope" if rope_tabs is not None else "inproj",
    )(*args)


def _fnet_kernel(u_ref, cs_ref, cc_ref, w_ref, o_ref, v_ref, *, scale):
    s = u_ref.shape[1]
    gw = FNET_GW

    @pl.when(pl.program_id(1) == 0)
    def _():
        for g in range(FNET_GROUPS):
            ug = u_ref[0, :, g * gw:(g + 1) * gw].astype(BF16)
            t = _dot(ug, cc_ref[...])
            v_ref[0:s, g * gw:(g + 1) * gw] = t[:, :gw].astype(BF16)
            v_ref[s:2 * s, g * gw:(g + 1) * gw] = t[:, gw:].astype(BF16)

    f = _dot(cs_ref[...], v_ref[...]) * scale
    for g in range(FNET_GROUPS):
        o_ref[0, :, g * gw:(g + 1) * gw] = _dot(f[:, g * gw:(g + 1) * gw].astype(BF16), w_ref[g])


def _fnet(o32, cs, ccsc, w, tn):
    b, s, _ = o32.shape
    scale = 1.0 / math.sqrt(s * FNET_GW)
    return pl.pallas_call(
        functools.partial(_fnet_kernel, scale=scale),
        grid=(b, s // tn),
        in_specs=[
            pl.BlockSpec((1, s, BRANCH_W), lambda bi, i: (bi, 0, 0)),
            pl.BlockSpec((tn, 2 * s), lambda bi, i: (i, 0)),
            pl.BlockSpec((FNET_GW, 2 * FNET_GW), lambda bi, i: (0, 0)),
            pl.BlockSpec((FNET_GROUPS, FNET_GW, FNET_GW), lambda bi, i: (0, 0, 0)),
        ],
        out_specs=pl.BlockSpec((1, tn, BRANCH_W), lambda bi, i: (bi, i, 0)),
        out_shape=jax.ShapeDtypeStruct((b, s, BRANCH_W), F32),
        scratch_shapes=[pltpu.VMEM((2 * s, BRANCH_W), BF16)],
        compiler_params=_cparams("arbitrary", "arbitrary"),
        name="fnet",
    )(o32, cs, ccsc, w)


def _pool_kernel(u_ref, w_ref, sc_ref, o_ref, pad_ref):
    s = u_ref.shape[1]
    gw = POOL_GW
    t = lax.broadcasted_iota(jnp.int32, (s, 1), 0)
    zeros = jnp.zeros((POOL_PAD, gw), F32)
    for g, win in enumerate(POOL_WINDOWS):
        u = u_ref[0, :, g * gw:(g + 1) * gw]
        pad_ref[0:POOL_PAD, :] = zeros
        pad_ref[POOL_PAD + s:2 * POOL_PAD + s, :] = zeros
        pad_ref[POOL_PAD:POOL_PAD + s, :] = u
        acc = None
        for j in range(-(win // 2), win - win // 2):
            term = pad_ref[POOL_PAD + j:POOL_PAD + j + s, :]
            acc = term if acc is None else acc + term
        lo = jnp.clip(t - win // 2, 0, s)
        hi = jnp.clip(t - win // 2 + win, 0, s)
        cnt = (hi - lo).astype(F32)
        pooled = (acc / cnt - u).astype(BF16)
        y = _dot(pooled, w_ref[g])
        o_ref[0, :, g * gw:(g + 1) * gw] = y * sc_ref[:, g * gw:(g + 1) * gw]


def _pool(o32, w, scale):
    b, s, _ = o32.shape
    return pl.pallas_call(
        _pool_kernel,
        grid=(b,),
        in_specs=[
            pl.BlockSpec((1, s, BRANCH_W), lambda bi: (bi, 0, 1)),
            pl.BlockSpec((len(POOL_WINDOWS), POOL_GW, POOL_GW), lambda bi: (0, 0, 0)),
            pl.BlockSpec((1, BRANCH_W), lambda bi: (0, 0)),
        ],
        out_specs=pl.BlockSpec((1, s, BRANCH_W), lambda bi: (bi, 0, 0)),
        out_shape=jax.ShapeDtypeStruct((b, s, BRANCH_W), F32),
        scratch_shapes=[pltpu.VMEM((s + 2 * POOL_PAD, POOL_GW), F32)],
        compiler_params=_cparams("arbitrary"),
        name="pool",
    )(o32, w, scale)


def _softmax_parts(scores):
    m = functools.reduce(jnp.maximum, [jnp.max(s, axis=-1, keepdims=True) for s in scores])
    es = [jnp.exp(s - m) for s in scores]
    den = functools.reduce(lambda a, c: a + c, [jnp.sum(e, axis=-1, keepdims=True) for e in es])
    return es, den


def _diff_kernel(*refs, n_seg, lambda_init):
    q_ref = refs[0]
    k_refs = refs[1:1 + n_seg]
    v_refs = refs[1 + n_seg:1 + 2 * n_seg]
    lam_ref, sub_ref, o_ref = refs[1 + 2 * n_seg:]
    lv = lam_ref[...]
    lam = (jnp.exp(jnp.sum(lv[0:1] * lv[1:2], axis=-1, keepdims=True))
           - jnp.exp(jnp.sum(lv[2:3] * lv[3:4], axis=-1, keepdims=True)) + lambda_init)
    tq = q_ref.shape[1]
    lane = lax.broadcasted_iota(jnp.int32, (tq, LANES), 1)
    zero = jnp.zeros((tq, LANES), BF16)
    q = q_ref[0] * (DIFF_HD ** -0.5)
    qs = jnp.concatenate([jnp.where(lane < DIFF_HD, q, zero), jnp.where(lane >= DIFF_HD, q, zero)], axis=0)
    es, den = _softmax_parts([_dot_nt(qs, k_ref[0]) for k_ref in k_refs])
    l1 = den[:tq]
    rho = lam * l1 / den[tq:]
    o = None
    for seg in range(n_seg):
        a = (es[seg][:tq] - rho * es[seg][tq:]).astype(BF16)
        pv = _dot(a, v_refs[seg][0])
        o = pv if o is None else o + pv
    o = o / l1
    o = o * lax.rsqrt(jnp.mean(o * o, axis=-1, keepdims=True) + SUBLN_EPS)
    o_ref[0] = o * sub_ref[...] * (1.0 - lambda_init)


def _diff_attn(q16, kv16_list, lam_vecs, subln, lambda_init, tq):
    b, sq, _ = q16.shape
    n_seg = len(kv16_list)
    nh = DIFF_HEADS
    in_specs = [pl.BlockSpec((1, tq, LANES), lambda bi, h, i: (bi, i, h))]
    in_specs += [pl.BlockSpec((1, kv.shape[1], LANES), lambda bi, h, i: (bi, 0, nh + h)) for kv in kv16_list]
    in_specs += [pl.BlockSpec((1, kv.shape[1], LANES), lambda bi, h, i: (bi, 0, 2 * nh + h)) for kv in kv16_list]
    in_specs += [
        pl.BlockSpec((4, DIFF_HD), lambda bi, h, i: (0, 0)),
        pl.BlockSpec((1, DIFF_VD), lambda bi, h, i: (0, 0)),
    ]
    return pl.pallas_call(
        functools.partial(_diff_kernel, n_seg=n_seg, lambda_init=lambda_init),
        grid=(b, nh, sq // tq),
        in_specs=in_specs,
        out_specs=pl.BlockSpec((1, tq, LANES), lambda bi, h, i: (bi, i, h)),
        out_shape=jax.ShapeDtypeStruct((b, sq, BRANCH_W), F32),
        compiler_params=_cparams("arbitrary", "arbitrary", "arbitrary"),
        name="diff_attn",
    )(q16, *kv16_list, *kv16_list, lam_vecs, subln)


DIFF_KEY_CHUNK = 512
DIFF_ROW_BLOCK = 64


def _key_chunks(seg_lens):
    chunks, col = [], 0
    for seg, n in enumerate(seg_lens):
        for r0 in range(0, n, DIFF_KEY_CHUNK):
            w = min(DIFF_KEY_CHUNK, n - r0)
            chunks.append((seg, r0, col, w))
            col += w
    return chunks


def _diff_pipe_kernel(q_ref, kc_ref, kl_ref, vc_ref, vl_ref, lam_ref, sub_ref, o_ref,
                      s_scr, e_scr, m_scr, l_scr, ma_scr, la_scr, *, lambda_init):
    g = pl.program_id(0)
    tq = q_ref.shape[1]
    k_refs = (kc_ref, kl_ref)
    v_refs = (vc_ref, vl_ref)
    chunks = _key_chunks([r.shape[1] for r in k_refs])

    @pl.when(g == 0)
    def _():
        s_scr[...] = jnp.zeros(s_scr.shape, F32)
        e_scr[...] = jnp.zeros(e_scr.shape, BF16)
        m_scr[...] = jnp.zeros(m_scr.shape, F32)
        l_scr[...] = jnp.ones(l_scr.shape, F32)

    lane = lax.broadcasted_iota(jnp.int32, (tq, LANES), 1)
    zero = jnp.zeros((tq, LANES), BF16)
    q = q_ref[0] * (DIFF_HD ** -0.5)
    qs = jnp.concatenate([jnp.where(lane < DIFF_HD, q, zero), jnp.where(lane >= DIFF_HD, q, zero)], axis=0)
    lv = lam_ref[...]
    lam = (jnp.exp(jnp.sum(lv[0:1] * lv[1:2], axis=-1, keepdims=True))
           - jnp.exp(jnp.sum(lv[2:3] * lv[3:4], axis=-1, keepdims=True)) + lambda_init)
    l1 = l_scr[0:tq, :]
    rho = (lam * l1 / l_scr[tq:2 * tq, :]).astype(BF16)
    acc = None
    n_rb = 2 * tq // DIFF_ROW_BLOCK
    for ci, (seg, r0, c0, w) in enumerate(chunks):
        first, last = ci == 0, ci == len(chunks) - 1
        a = jnp.concatenate(
            [e_scr[0:tq, c0 + j * LANES:c0 + (j + 1) * LANES] - rho * e_scr[tq:2 * tq, c0 + j * LANES:c0 + (j + 1) * LANES]
             for j in range(w // LANES)], axis=1)
        pv = _dot(a, v_refs[seg][0, r0:r0 + w, :])
        acc = pv if acc is None else acc + pv
        for rb in range(n_rb):
            rows = slice(rb * DIFF_ROW_BLOCK, (rb + 1) * DIFF_ROW_BLOCK)
            m_prev = m_scr[rows, :]
            l_run = None if first else la_scr[rows, :]
            for j in range(w // LANES):
                cols = slice(c0 + j * LANES, c0 + (j + 1) * LANES)
                ej = jnp.exp(s_scr[rows, cols] - m_prev)
                l_run = ej if l_run is None else l_run + ej
                e_scr[rows, cols] = ej.astype(BF16)
            if last:
                l_scr[rows, :] = jnp.broadcast_to(jnp.sum(l_run, axis=1, keepdims=True), l_run.shape)
            else:
                la_scr[rows, :] = l_run
        s_scr[:, c0:c0 + w] = _dot_nt(qs, k_refs[seg][0, r0:r0 + w, :])
        for rb in range(n_rb):
            rows = slice(rb * DIFF_ROW_BLOCK, (rb + 1) * DIFF_ROW_BLOCK)
            m_run = None if first else ma_scr[rows, :]
            for j in range(w // LANES):
                sj = s_scr[rows, c0 + j * LANES:c0 + (j + 1) * LANES]
                m_run = sj if m_run is None else jnp.maximum(m_run, sj)
            if last:
                m_scr[rows, :] = jnp.broadcast_to(jnp.max(m_run, axis=1, keepdims=True), m_run.shape)
            else:
                ma_scr[rows, :] = m_run

    o = acc / l1
    o = o * lax.rsqrt(jnp.mean(o * o, axis=-1, keepdims=True) + SUBLN_EPS)
    o_ref[0] = o * sub_ref[...] * (1.0 - lambda_init)


def _diff_attn_lat(lat16, ctx16, lam_vecs, subln, lambda_init, tq):
    b, s, _ = lat16.shape
    sc = ctx16.shape[1]
    nh = DIFF_HEADS
    nt = s // tq
    n_tiles = b * nh * nt

    def tile(g, lag):
        t = jnp.clip(g - lag, 0, n_tiles - 1)
        return t // (nh * nt), (t // nt) % nh, t % nt

    def q_map(g):
        bi, h, i = tile(g, 0)
        return (bi, i, h)

    def k_map(g):
        bi, h, _ = tile(g, 0)
        return (bi, 0, nh + h)

    def v_map(g):
        bi, h, _ = tile(g, 2)
        return (bi, 0, 2 * nh + h)

    def o_map(g):
        bi, h, i = tile(g, 2)
        return (bi, i, h)

    nk = sc + s
    stat = pltpu.VMEM((2 * tq, LANES), F32)
    return pl.pallas_call(
        functools.partial(_diff_pipe_kernel, lambda_init=lambda_init),
        grid=(n_tiles + 2,),
        in_specs=[
            pl.BlockSpec((1, tq, LANES), q_map),
            pl.BlockSpec((1, sc, LANES), k_map),
            pl.BlockSpec((1, s, LANES), k_map),
            pl.BlockSpec((1, sc, LANES), v_map),
            pl.BlockSpec((1, s, LANES), v_map),
            pl.BlockSpec((4, DIFF_HD), lambda g: (0, 0)),
            pl.BlockSpec((1, DIFF_VD), lambda g: (0, 0)),
        ],
        out_specs=pl.BlockSpec((1, tq, LANES), o_map),
        out_shape=jax.ShapeDtypeStruct((b, s, BRANCH_W), F32),
        scratch_shapes=[pltpu.VMEM((2 * tq, nk), F32), pltpu.VMEM((2 * tq, nk), BF16), stat, stat, stat, stat],
        compiler_params=_cparams("arbitrary"),
        name="diff_attn_lat",
    )(lat16, ctx16, lat16, ctx16, lat16, lam_vecs, subln)


NA_Q_BLK, NA_K_BLK, NA_V_BLK = 3, 4, 5


def _na_heads(q_ref, segs, o_ref):
    tq = q_ref.shape[1]
    lane = lax.broadcasted_iota(jnp.int32, (tq, LANES), 1)
    zero = jnp.zeros((tq, LANES), BF16)
    for hp in range(NA_HEADS // 2):
        q = q_ref[0, :, hp * LANES:(hp + 1) * LANES] * (NA_HD ** -0.5)
        qs = jnp.concatenate([jnp.where(lane < NA_HD, q, zero), jnp.where(lane >= NA_HD, q, zero)], axis=0)
        scores = []
        for k_of, _, bias_of in segs:
            sc = _dot_nt(qs, k_of(hp))
            if bias_of is not None:
                sc = sc + bias_of(hp)
            scores.append(sc)
        es, den = _softmax_parts(scores)
        o = None
        for (_, v_of, _), e in zip(segs, es):
            pv = _dot(e.astype(BF16), v_of(hp))
            o = pv if o is None else o + pv
        o = o / den
        o_ref[0, :, hp * LANES:(hp + 1) * LANES] = jnp.where(lane < NA_HD, o[:tq], o[tq:])


def _na_lat_kernel(q_ref, k_ref, v_ref, kc_ref, vc_ref, bias_ref, o_ref, *, rows):
    blk = pl.program_id(1)
    kstart = jnp.clip(blk * NA_ROWS_PER_BLOCK - NA_KH // 2, 0, rows - NA_KEY_ROWS)
    off = pl.multiple_of(kstart * GRID_W, GRID_W)
    nk = NA_KEY_ROWS * GRID_W
    pair = lambda hp: slice(hp * LANES, (hp + 1) * LANES)
    window = (lambda hp: k_ref[0, pl.ds(off, nk), pair(hp)],
              lambda hp: v_ref[0, pl.ds(off, nk), pair(hp)],
              lambda hp: jnp.concatenate([bias_ref[0, 2 * hp], bias_ref[0, 2 * hp + 1]], axis=0))
    context = (lambda hp: kc_ref[0, :, pair(hp)], lambda hp: vc_ref[0, :, pair(hp)], None)
    _na_heads(q_ref, [window, context], o_ref)


def _na_lat(lat16, ctx16, bias):
    b, s, _ = lat16.shape
    sc = ctx16.shape[1]
    rows = s // GRID_W
    nblk = rows // NA_ROWS_PER_BLOCK
    tq = NA_ROWS_PER_BLOCK * GRID_W
    nk = NA_KEY_ROWS * GRID_W
    bw = BRANCH_W

    def bias_map(bi, blk):
        case = jnp.where(blk == 0, 0, jnp.where(blk == nblk - 1, 2, 1))
        return (case, 0, 0, 0)

    return pl.pallas_call(
        functools.partial(_na_lat_kernel, rows=rows),
        grid=(b, nblk),
        in_specs=[
            pl.BlockSpec((1, tq, bw), lambda bi, blk: (bi, blk, NA_Q_BLK)),
            pl.BlockSpec((1, s, bw), lambda bi, blk: (bi, 0, NA_K_BLK)),
            pl.BlockSpec((1, s, bw), lambda bi, blk: (bi, 0, NA_V_BLK)),
            pl.BlockSpec((1, sc, bw), lambda bi, blk: (bi, 0, NA_K_BLK)),
            pl.BlockSpec((1, sc, bw), lambda bi, blk: (bi, 0, NA_V_BLK)),
            pl.BlockSpec((1, NA_HEADS, tq, nk), bias_map),
        ],
        out_specs=pl.BlockSpec((1, tq, bw), lambda bi, blk: (bi, blk, 0)),
        out_shape=jax.ShapeDtypeStruct((b, s, bw), F32),
        compiler_params=_cparams("arbitrary", "arbitrary"),
        name="na_lat",
    )(lat16, lat16, lat16, ctx16, ctx16, bias)


def _na_ctx_kernel(q_ref, k_ref, v_ref, o_ref):
    pair = lambda hp: slice(hp * LANES, (hp + 1) * LANES)
    _na_heads(q_ref, [(lambda hp: k_ref[0, :, pair(hp)], lambda hp: v_ref[0, :, pair(hp)], None)], o_ref)


def _na_ctx(ctx16):
    b, sc, _ = ctx16.shape
    bw = BRANCH_W
    return pl.pallas_call(
        _na_ctx_kernel,
        grid=(b,),
        in_specs=[
            pl.BlockSpec((1, sc, bw), lambda bi: (bi, 0, NA_Q_BLK)),
            pl.BlockSpec((1, sc, bw), lambda bi: (bi, 0, NA_K_BLK)),
            pl.BlockSpec((1, sc, bw), lambda bi: (bi, 0, NA_V_BLK)),
        ],
        out_specs=pl.BlockSpec((1, sc, bw), lambda bi: (bi, 0, 0)),
        out_shape=jax.ShapeDtypeStruct((b, sc, bw), F32),
        compiler_params=_cparams("arbitrary"),
        name="na_ctx",
    )(ctx16, ctx16, ctx16)


def _na_bias(tab, rows):
    w = GRID_W
    lead = tab.shape[:-1]
    left = w - NA_KW
    g = jnp.pad(tab, [(0, 0)] * len(lead) + [(left, 2 * w - left - (2 * NA_KW - 1))])
    g = jnp.broadcast_to(g[..., None, :], lead + (w, 2 * w)).reshape(lead + (2 * w * w,))
    e = g[..., :w * (2 * w - 1)].reshape(lead + (w, 2 * w - 1))[..., w - 1:]
    col = np.arange(w)
    cstart = np.clip(col - NA_KW // 2, 0, w - NA_KW)
    col_ok = (col[None, :] >= cstart[:, None]) & (col[None, :] < cstart[:, None] + NA_KW)
    e = jnp.where(col_ok, e, MASK_VALUE)
    masked = jnp.full(lead[:-1] + (w, w), MASK_VALUE, F32)
    r_blk = NA_ROWS_PER_BLOCK
    cases = []
    for r0, kstart in ((0, 0), (r_blk, r_blk - NA_KH // 2), (rows - r_blk, rows - NA_KEY_ROWS)):
        q_rows = []
        for qi in range(r_blk):
            r = r0 + qi
            rs = min(max(r - NA_KH // 2, 0), rows - NA_KH)
            blocks = []
            for ki in range(NA_KEY_ROWS):
                kr = kstart + ki
                blocks.append(e[..., kr - r + NA_KH - 1, :, :] if rs <= kr < rs + NA_KH else masked)
            q_rows.append(jnp.concatenate(blocks, axis=-1))
        cases.append(jnp.concatenate(q_rows, axis=-2))
    return jnp.stack(cases, axis=1)


def _final_kernel(x_ref, mod_ref, yf_ref, yd_ref, yp_ref, yn_ref, wg_ref, bg_ref, wb_ref, wo_ref,
                  g_ref, b_ref, o_ref, *, alpha):
    x = x_ref[0]
    xm = _modulated_ln(x, mod_ref).astype(BF16)
    bw = BRANCH_W
    d = D_MODEL
    acc = None
    for i, y_ref in enumerate((yf_ref, yd_ref, yp_ref, yn_ref)):
        zg = _dot(xm, wg_ref[:, i * bw:(i + 1) * bw]) + bg_ref[:, i * bw:(i + 1) * bw]
        gated = (y_ref[0] * (zg * _sigmoid(zg))).astype(BF16)
        proj = _dot(gated, wb_ref[i])
        c0 = N_BRANCH * bw + i * d
        zm = _dot(xm, wg_ref[:, c0:c0 + d]) + bg_ref[:, c0:c0 + d]
        term = _sigmoid(zm) * proj
        acc = term if acc is None else acc + term
    out = _dot(acc.astype(BF16), wo_ref[...])
    h = alpha * x + mod_ref[0, 2:3, :] * out
    mu = jnp.mean(h, axis=-1, keepdims=True)
    hc = h - mu
    var = jnp.mean(hc * hc, axis=-1, keepdims=True)
    o_ref[0] = hc * lax.rsqrt(var + LN_EPS) * g_ref[...] + b_ref[...]


def _final(x, mod, per_batch_mod, ys, wg, bg, wb, wo, ln_g, ln_b, alpha, tm):
    b, s, d = x.shape
    ng = wg.shape[1]
    mod_map = (lambda bi, i: (bi, 0, 0)) if per_batch_mod else (lambda bi, i: (0, 0, 0))
    tile = lambda w: pl.BlockSpec((1, tm, w), lambda bi, i: (bi, i, 0))
    const2 = lambda shape: pl.BlockSpec(shape, lambda bi, i: (0, 0), pipeline_mode=pl.Buffered(1))
    return pl.pallas_call(
        functools.partial(_final_kernel, alpha=alpha),
        grid=(b, s // tm),
        in_specs=[
            tile(d),
            pl.BlockSpec((1, 3, d), mod_map),
            tile(BRANCH_W), tile(BRANCH_W), tile(BRANCH_W), tile(BRANCH_W),
            const2((d, ng)),
            const2((1, ng)),
            pl.BlockSpec((N_BRANCH, BRANCH_W, d), lambda bi, i: (0, 0, 0), pipeline_mode=pl.Buffered(1)),
            const2((d, d)),
            const2((1, d)),
            const2((1, d)),
        ],
        out_specs=tile(d),
        out_shape=jax.ShapeDtypeStruct((b, s, d), F32),
        compiler_params=_cparams("arbitrary", "arbitrary"),
        name="gate_merge_out",
    )(x, mod, *ys, wg, bg, wb, wo, ln_g, ln_b)


def _rope_tables(s):
    t = jnp.arange(s)
    rows = (t // GRID_W).astype(F32)
    cols = (t % GRID_W).astype(F32)
    nf = DIFF_HD // 4
    inv = ROPE_BASE ** (-jnp.arange(nf, dtype=F32) / nf)
    ang = jnp.stack([rows[:, None] * inv, cols[:, None] * inv], axis=1)
    cos = jnp.cos(ang)
    sin = jnp.sin(ang)
    cos64 = jnp.concatenate([cos[:, 0], cos[:, 0], cos[:, 1], cos[:, 1]], axis=-1)
    sin64 = jnp.concatenate([-sin[:, 0], sin[:, 0], -sin[:, 1], sin[:, 1]], axis=-1)
    rep = LANES // DIFF_HD
    return jnp.tile(cos64, (1, rep)), jnp.tile(sin64, (1, rep))


def _dft_tables(n):
    idx = jnp.arange(n, dtype=jnp.int32)
    m = (idx[:, None] * idx[None, :]) % n
    ang = m.astype(F32) * (2.0 * math.pi / n)
    return jnp.cos(ang), jnp.sin(ang)


def _split_cols(w, sizes):
    out, acc = [], 0
    for sz in sizes:
        out.append(w[..., acc:acc + sz])
        acc += sz
    return out


def kernel(x, c, ctx, c_ctx, w_mod, b_mod, w_in, b_in, fnet_w, diff_lam, diff_subln, pool_w, pool_scale,
           na_bias, w_branch, w_out, ln_g, ln_b):
    bsz, s, d = x.shape
    s_ctx = ctx.shape[1]
    rows = s // GRID_W
    assert d == D_MODEL and s % (GRID_W * NA_ROWS_PER_BLOCK) == 0 and rows >= NA_KEY_ROWS
    alpha = (2.0 * DEPTH) ** 0.25

    z = _split_cols(w_in, (BRANCH_W,) * 12 + (N_BRANCH * D_MODEL,))
    zb = _split_cols(b_in, (BRANCH_W,) * 12 + (N_BRANCH * D_MODEL,))
    order_a = (0, 6, 2, 3, 4, 8, 9, 10)
    order_g = (1, 5, 7, 11, 12)
    wa = jnp.concatenate([z[i] for i in order_a], axis=-1).astype(BF16)
    ba = jnp.concatenate([zb[i] for i in order_a], axis=-1)
    wg = jnp.concatenate([z[i] for i in order_g], axis=-1).astype(BF16)
    bg = jnp.concatenate([zb[i] for i in order_g], axis=-1)
    wb16 = w_branch.astype(BF16)
    wo16 = w_out.astype(BF16)
    fw16 = fnet_w.astype(BF16)
    pw16 = pool_w.astype(BF16)

    rope_tabs = _rope_tables(s)
    cs_tabs = {}
    for n in {s, s_ctx}:
        cn, sn = _dft_tables(n)
        cs_tabs[n] = jnp.concatenate([cn, -sn], axis=1).astype(BF16)
    cc, sc_ = _dft_tables(FNET_GW)
    ccsc = jnp.concatenate([cc, sc_], axis=1).astype(BF16)
    na_bias_all = _na_bias(na_bias, rows)

    r_pad = -(-(bsz + 1) // 8) * 8
    cc_all = jnp.zeros((r_pad, d), F32).at[:bsz].set(c).at[bsz].set(c_ctx)
    mod_all = _modulation(cc_all, w_mod, b_mod)

    tm_lat = min(512, s)
    tm_fin = min(512, s)
    tq_diff = min(256, s)
    tn_fnet = min(512, s)

    h, hc = x, ctx
    for l in range(DEPTH):
        lambda_init = 0.8 - 0.6 * math.exp(-0.3 * l)
        need_ctx = l < DEPTH - 1
        mod_lat = mod_all[l, :bsz].reshape(bsz, 3, d)
        mod_ctx = mod_all[l, bsz:bsz + 1].reshape(1, 3, d)

        lat32, lat16 = _inproj(h, mod_lat, True, wa[l], ba[l][None], rope_tabs, tm_lat)
        ctx32, ctx16 = _inproj(hc, mod_ctx, False, wa[l], ba[l][None], None, s_ctx)

        lam = diff_lam[l]
        sub = diff_subln[l][None]
        psc = pool_scale[l][None]
        y_f = _fnet(lat32, cs_tabs[s], ccsc, fw16[l], tn_fnet)
        y_d = _diff_attn_lat(lat16, ctx16, lam, sub, lambda_init, tq_diff)
        y_p = _pool(lat32, pw16[l], psc)
        y_n = _na_lat(lat16, ctx16, na_bias_all[l])
        h_new = _final(h, mod_lat, True, (y_f, y_d, y_p, y_n), wg[l], bg[l][None], wb16[l], wo16[l],
                       ln_g[l][None], ln_b[l][None], alpha, tm_fin)
        if need_ctx:
            y_fc = _fnet(ctx32, cs_tabs[s_ctx], ccsc, fw16[l], s_ctx)
            y_dc = _diff_attn(ctx16, [ctx16], lam, sub, lambda_init, s_ctx)
            y_pc = _pool(ctx32, pw16[l], psc)
            y_nc = _na_ctx(ctx16)
            hc = _final(hc, mod_ctx, False, (y_fc, y_dc, y_pc, y_nc), wg[l], bg[l][None], wb16[l], wo16[l],
                        ln_g[l][None], ln_b[l][None], alpha, s_ctx)
        h = h_new
    return h
```

```python
import functools
import math

import numpy as np
import jax
import jax.numpy as jnp
from jax import lax
from jax.experimental import pallas as pl
from jax.experimental.pallas import tpu as pltpu

D_MODEL = 1024
DEPTH = 4
GRID_W = 64
N_BRANCH = 4
BRANCH_W = 512
FNET_GROUPS = 4
FNET_GW = BRANCH_W // FNET_GROUPS
DIFF_HEADS = 4
DIFF_HD = 64
DIFF_VD = 2 * DIFF_HD
POOL_WINDOWS = (2, 4, 8, 16)
POOL_GW = BRANCH_W // len(POOL_WINDOWS)
NA_HEADS = 8
NA_HD = BRANCH_W // NA_HEADS
NA_KH = 8
NA_KW = 16
ROPE_BASE = 10000.0
LN_EPS = 1e-6
SUBLN_EPS = 1e-5

LANES = 128
POOL_PAD = 16
NA_ROWS_PER_BLOCK = 4
NA_KEY_ROWS = NA_ROWS_PER_BLOCK + NA_KH - 1
MASK_VALUE = -1e30
VMEM_LIMIT = 56 * 1024 * 1024

F32 = jnp.float32
BF16 = jnp.bfloat16


def _cparams(*sem):
    return pltpu.CompilerParams(dimension_semantics=sem, vmem_limit_bytes=VMEM_LIMIT)


def _dot(a, b):
    return jnp.dot(a, b, preferred_element_type=F32)


def _dot_nt(a, b):
    return lax.dot_general(a, b, (((1,), (1,)), ((), ())), preferred_element_type=F32)


def _sigmoid(x):
    return 1.0 / (1.0 + jnp.exp(-x))


def _modulated_ln(x, mod_ref):
    mu = jnp.mean(x, axis=-1, keepdims=True)
    xc = x - mu
    var = jnp.mean(xc * xc, axis=-1, keepdims=True)
    y = xc * lax.rsqrt(var + LN_EPS)
    return y * (1.0 + mod_ref[0, 1:2, :]) + mod_ref[0, 0:1, :]


def _mod_kernel(c_ref, w_ref, b_ref, o_ref):
    c = c_ref[...]
    s = (c * _sigmoid(c)).astype(BF16)
    o_ref[0] = _dot(s, w_ref[0].astype(BF16)) + b_ref[0]


def _modulation(cc, w_mod, b_mod):
    n_l, d, d3 = w_mod.shape
    r = cc.shape[0]
    tn = d
    return pl.pallas_call(
        _mod_kernel,
        grid=(n_l, d3 // tn),
        in_specs=[
            pl.BlockSpec((r, d), lambda l, j: (0, 0)),
            pl.BlockSpec((1, d, tn), lambda l, j: (l, 0, j)),
            pl.BlockSpec((1, 1, tn), lambda l, j: (l, 0, j)),
        ],
        out_specs=pl.BlockSpec((1, r, tn), lambda l, j: (l, 0, j)),
        out_shape=jax.ShapeDtypeStruct((n_l, r, d3), F32),
        compiler_params=_cparams("arbitrary", "arbitrary"),
        name="modulation",
    )(cc, w_mod, b_mod.reshape(n_l, 1, d3))


def _rope(r, cos_ref, sin_ref):
    lane = lax.broadcasted_iota(jnp.int32, (r.shape[0], LANES), 1)
    first = (lane % 32) < 16
    cos = cos_ref[...]
    sin = sin_ref[...]
    outs = []
    for k in range(r.shape[1] // LANES):
        xk = r[:, k * LANES:(k + 1) * LANES]
        partner = jnp.where(first, pltpu.roll(xk, LANES - 16, 1), pltpu.roll(xk, 16, 1))
        outs.append(xk * cos + partner * sin)
    return jnp.concatenate(outs, axis=1)


def _inproj_kernel(*refs, rope):
    if rope:
        x_ref, mod_ref, w_ref, b_ref, cos_ref, sin_ref, o32_ref, o16_ref = refs
    else:
        x_ref, mod_ref, w_ref, b_ref, o32_ref, o16_ref = refs
    xm = _modulated_ln(x_ref[0], mod_ref).astype(BF16)
    bw = BRANCH_W
    for j in range(2):
        o32_ref[0, :, j * bw:(j + 1) * bw] = _dot(xm, w_ref[:, j * bw:(j + 1) * bw]) + b_ref[:, j * bw:(j + 1) * bw]
    for j in range(6):
        c0 = (2 + j) * bw
        r = _dot(xm, w_ref[:, c0:c0 + bw]) + b_ref[:, c0:c0 + bw]
        if rope and j in (0, 1):
            r = _rope(r, cos_ref, sin_ref)
        o16_ref[0, :, j * bw:(j + 1) * bw] = r.astype(BF16)


def _inproj(x, mod, per_batch_mod, wa, ba, rope_tabs, tm):
    b, s, d = x.shape
    na = wa.shape[1]
    mod_map = (lambda bi, i: (bi, 0, 0)) if per_batch_mod else (lambda bi, i: (0, 0, 0))
    in_specs = [
        pl.BlockSpec((1, tm, d), lambda bi, i: (bi, i, 0)),
        pl.BlockSpec((1, 3, d), mod_map),
        pl.BlockSpec((d, na), lambda bi, i: (0, 0)),
        pl.BlockSpec((1, na), lambda bi, i: (0, 0)),
    ]
    args = [x, mod, wa, ba]
    if rope_tabs is not None:
        in_specs += [pl.BlockSpec((tm, LANES), lambda bi, i: (i, 0))] * 2
        args += list(rope_tabs)
    return pl.pallas_call(
        functools.partial(_inproj_kernel, rope=rope_tabs is not None),
        grid=(b, s // tm),
        in_specs=in_specs,
        out_specs=[
            pl.BlockSpec((1, tm, 2 * BRANCH_W), lambda bi, i: (bi, i, 0)),
            pl.BlockSpec((1, tm, 6 * BRANCH_W), lambda bi, i: (bi, i, 0)),
        ],
        out_shape=[
            jax.ShapeDtypeStruct((b, s, 2 * BRANCH_W), F32),
            jax.ShapeDtypeStruct((b, s, 6 * BRANCH_W), BF16),
        ],
        compiler_params=_cparams("arbitrary", "arbitrary"),
        name="inproj_rope" if rope_tabs is not None else "inproj",
    )(*args)


FFT_PAD = 4
FFT_GROUPS_PER_STEP = 2
FFT_UNROLL = 8


def _fft_kernel(u_ref, f1_ref, f2_ref, tw_ref, cc_ref, w_ref, o_ref, t_scr, z_scr, *, r, scale):
    p = r + FFT_PAD
    ng = FFT_GROUPS_PER_STEP
    lanes = lambda g: slice(g * LANES, (g + 1) * LANES)

    def transpose_in(n1, c):
        blk = u_ref[0, pl.ds(pl.multiple_of(n1 * r, r), r), :]
        for g in range(ng):
            t_scr[g, pl.ds(n1, r, stride=p), :] = blk[:, lanes(g)]
        return c

    lax.fori_loop(0, r, transpose_in, 0, unroll=FFT_UNROLL)

    def stage1(n2, c):
        m = jnp.concatenate([t_scr[g, pl.ds(n2 * p, r), :] for g in range(ng)], axis=1).astype(BF16)
        y = _dot(f1_ref[...], m)
        tr = tw_ref[n2, 0]
        ti = tw_ref[n2, 1]
        for g in range(ng):
            a = y[:r, lanes(g)]
            b = y[r:, lanes(g)]
            z_scr[0, g, pl.ds(n2, r, stride=p), :] = a * tr - b * ti
            z_scr[1, g, pl.ds(n2, r, stride=p), :] = a * ti + b * tr
        return c

    lax.fori_loop(0, r, stage1, 0, unroll=FFT_UNROLL)

    def stage2(blk, c):
        xs = []
        for i in range(FFT_UNROLL):
            k1 = blk * FFT_UNROLL + i
            zz = jnp.concatenate(
                [jnp.concatenate([z_scr[ri, g, pl.ds(k1 * p, r), :] for g in range(ng)], axis=1) for ri in range(2)],
                axis=0).astype(BF16)
            xs.append(_dot(f2_ref[...], zz))
        for g in range(ng):
            xg = jnp.concatenate([jnp.concatenate([x[:r, lanes(g)], x[r:, lanes(g)]], axis=1) for x in xs], axis=0)
            f = _dot(xg.astype(BF16), cc_ref[...]) * scale
            y = _dot(f.astype(BF16), w_ref[g])
            for i in range(FFT_UNROLL):
                t_scr[g, pl.ds(blk * FFT_UNROLL + i, r, stride=p), :] = y[i * r:(i + 1) * r]
        return c

    lax.fori_loop(0, r // FFT_UNROLL, stage2, 0)

    def copy_out(k2, c):
        for g in range(ng):
            o_ref[0, pl.ds(pl.multiple_of(k2 * r, r), r), lanes(g)] = t_scr[g, pl.ds(k2 * p, r), :]
        return c

    lax.fori_loop(0, r, copy_out, 0, unroll=FFT_UNROLL)


def _fft_tables(s):
    r = math.isqrt(s)
    assert r * r == s and r % FFT_UNROLL == 0
    cr, sr = _dft_tables(r)
    f1 = jnp.concatenate([cr, -sr], axis=0).astype(BF16)
    f2 = jnp.concatenate([jnp.concatenate([cr, sr], axis=1), jnp.concatenate([-sr, cr], axis=1)], axis=0).astype(BF16)
    idx = jnp.arange(r, dtype=jnp.int32)
    ang = (idx[:, None] * idx[None, :]).astype(F32) * (2.0 * math.pi / s)
    tw = jnp.stack([jnp.cos(ang), -jnp.sin(ang)], axis=1)
    tw = jnp.broadcast_to(tw[..., None], tw.shape + (LANES,))
    return f1, f2, tw


def _fnet_fft(o32, tabs, ccsc, w):
    b, s, _ = o32.shape
    f1, f2, tw = tabs
    r = f1.shape[1]
    p = r + FFT_PAD
    ng = FFT_GROUPS_PER_STEP
    scale = 1.0 / math.sqrt(s * FNET_GW)
    return pl.pallas_call(
        functools.partial(_fft_kernel, r=r, scale=scale),
        grid=(b, FNET_GROUPS // ng),
        in_specs=[
            pl.BlockSpec((1, s, ng * LANES), lambda bi, j: (bi, 0, j)),
            pl.BlockSpec((2 * r, r), lambda bi, j: (0, 0)),
            pl.BlockSpec((2 * r, 2 * r), lambda bi, j: (0, 0)),
            pl.BlockSpec((r, 2, r, LANES), lambda bi, j: (0, 0, 0, 0)),
            pl.BlockSpec((2 * FNET_GW, FNET_GW), lambda bi, j: (0, 0)),
            pl.BlockSpec((ng, FNET_GW, FNET_GW), lambda bi, j: (j, 0, 0)),
        ],
        out_specs=pl.BlockSpec((1, s, ng * LANES), lambda bi, j: (bi, 0, j)),
        out_shape=jax.ShapeDtypeStruct((b, s, BRANCH_W), F32),
        scratch_shapes=[pltpu.VMEM((ng, r * p, LANES), F32), pltpu.VMEM((2, ng, r * p, LANES), F32)],
        compiler_params=_cparams("arbitrary", "arbitrary"),
        name="fnet_fft",
    )(o32, f1, f2, tw, ccsc, w)


def _pool_kernel(u_ref, w_ref, sc_ref, o_ref, pad_ref):
    s = u_ref.shape[1]
    gw = POOL_GW
    t = lax.broadcasted_iota(jnp.int32, (s, 1), 0)
    zeros = jnp.zeros((POOL_PAD, gw), F32)
    for g, win in enumerate(POOL_WINDOWS):
        u = u_ref[0, :, g * gw:(g + 1) * gw]
        pad_ref[0:POOL_PAD, :] = zeros
        pad_ref[POOL_PAD + s:2 * POOL_PAD + s, :] = zeros
        pad_ref[POOL_PAD:POOL_PAD + s, :] = u
        acc = None
        for j in range(-(win // 2), win - win // 2):
            term = pad_ref[POOL_PAD + j:POOL_PAD + j + s, :]
            acc = term if acc is None else acc + term
        lo = jnp.clip(t - win // 2, 0, s)
        hi = jnp.clip(t - win // 2 + win, 0, s)
        cnt = (hi - lo).astype(F32)
        pooled = (acc / cnt - u).astype(BF16)
        y = _dot(pooled, w_ref[g])
        o_ref[0, :, g * gw:(g + 1) * gw] = y * sc_ref[:, g * gw:(g + 1) * gw]


def _pool(o32, w, scale):
    b, s, _ = o32.shape
    return pl.pallas_call(
        _pool_kernel,
        grid=(b,),
        in_specs=[
            pl.BlockSpec((1, s, BRANCH_W), lambda bi: (bi, 0, 1)),
            pl.BlockSpec((len(POOL_WINDOWS), POOL_GW, POOL_GW), lambda bi: (0, 0, 0)),
            pl.BlockSpec((1, BRANCH_W), lambda bi: (0, 0)),
        ],
        out_specs=pl.BlockSpec((1, s, BRANCH_W), lambda bi: (bi, 0, 0)),
        out_shape=jax.ShapeDtypeStruct((b, s, BRANCH_W), F32),
        scratch_shapes=[pltpu.VMEM((s + 2 * POOL_PAD, POOL_GW), F32)],
        compiler_params=_cparams("arbitrary"),
        name="pool",
    )(o32, w, scale)


def _softmax_parts(scores):
    m = functools.reduce(jnp.maximum, [jnp.max(s, axis=-1, keepdims=True) for s in scores])
    es = [jnp.exp(s - m) for s in scores]
    den = functools.reduce(lambda a, c: a + c, [jnp.sum(e, axis=-1, keepdims=True) for e in es])
    return es, den


def _diff_kernel(*refs, n_seg, lambda_init):
    q_ref = refs[0]
    k_refs = refs[1:1 + n_seg]
    v_refs = refs[1 + n_seg:1 + 2 * n_seg]
    lam_ref, sub_ref, o_ref = refs[1 + 2 * n_seg:]
    lv = lam_ref[...]
    lam = (jnp.exp(jnp.sum(lv[0:1] * lv[1:2], axis=-1, keepdims=True))
           - jnp.exp(jnp.sum(lv[2:3] * lv[3:4], axis=-1, keepdims=True)) + lambda_init)
    tq = q_ref.shape[1]
    lane = lax.broadcasted_iota(jnp.int32, (tq, LANES), 1)
    zero = jnp.zeros((tq, LANES), BF16)
    q = q_ref[0] * (DIFF_HD ** -0.5)
    qs = jnp.concatenate([jnp.where(lane < DIFF_HD, q, zero), jnp.where(lane >= DIFF_HD, q, zero)], axis=0)
    es, den = _softmax_parts([_dot_nt(qs, k_ref[0]) for k_ref in k_refs])
    l1 = den[:tq]
    rho = lam * l1 / den[tq:]
    o = None
    for seg in range(n_seg):
        a = (es[seg][:tq] - rho * es[seg][tq:]).astype(BF16)
        pv = _dot(a, v_refs[seg][0])
        o = pv if o is None else o + pv
    o = o / l1
    o = o * lax.rsqrt(jnp.mean(o * o, axis=-1, keepdims=True) + SUBLN_EPS)
    o_ref[0] = o * sub_ref[...] * (1.0 - lambda_init)


def _diff_attn(q16, kv16_list, lam_vecs, subln, lambda_init, tq):
    b, sq, _ = q16.shape
    n_seg = len(kv16_list)
    nh = DIFF_HEADS
    in_specs = [pl.BlockSpec((1, tq, LANES), lambda bi, h, i: (bi, i, h))]
    in_specs += [pl.BlockSpec((1, kv.shape[1], LANES), lambda bi, h, i: (bi, 0, nh + h)) for kv in kv16_list]
    in_specs += [pl.BlockSpec((1, kv.shape[1], LANES), lambda bi, h, i: (bi, 0, 2 * nh + h)) for kv in kv16_list]
    in_specs += [
        pl.BlockSpec((4, DIFF_HD), lambda bi, h, i: (0, 0)),
        pl.BlockSpec((1, DIFF_VD), lambda bi, h, i: (0, 0)),
    ]
    return pl.pallas_call(
        functools.partial(_diff_kernel, n_seg=n_seg, lambda_init=lambda_init),
        grid=(b, nh, sq // tq),
        in_specs=in_specs,
        out_specs=pl.BlockSpec((1, tq, LANES), lambda bi, h, i: (bi, i, h)),
        out_shape=jax.ShapeDtypeStruct((b, sq, BRANCH_W), F32),
        compiler_params=_cparams("arbitrary", "arbitrary", "arbitrary"),
        name="diff_attn",
    )(q16, *kv16_list, *kv16_list, lam_vecs, subln)


DIFF_KEY_CHUNK = 512
DIFF_ROW_BLOCK = 64


def _key_chunks(seg_lens):
    chunks, col = [], 0
    for seg, n in enumerate(seg_lens):
        for r0 in range(0, n, DIFF_KEY_CHUNK):
            w = min(DIFF_KEY_CHUNK, n - r0)
            chunks.append((seg, r0, col, w))
            col += w
    return chunks


def _diff_pipe_kernel(q_ref, kc_ref, kl_ref, vc_ref, vl_ref, lam_ref, sub_ref, o_ref,
                      s_scr, e_scr, m_scr, l_scr, ma_scr, la_scr, *, lambda_init):
    g = pl.program_id(0)
    tq = q_ref.shape[1]
    k_refs = (kc_ref, kl_ref)
    v_refs = (vc_ref, vl_ref)
    chunks = _key_chunks([r.shape[1] for r in k_refs])

    @pl.when(g == 0)
    def _():
        s_scr[...] = jnp.zeros(s_scr.shape, F32)
        e_scr[...] = jnp.zeros(e_scr.shape, BF16)
        m_scr[...] = jnp.zeros(m_scr.shape, F32)
        l_scr[...] = jnp.ones(l_scr.shape, F32)

    lane = lax.broadcasted_iota(jnp.int32, (tq, LANES), 1)
    zero = jnp.zeros((tq, LANES), BF16)
    q = q_ref[0] * (DIFF_HD ** -0.5)
    qs = jnp.concatenate([jnp.where(lane < DIFF_HD, q, zero), jnp.where(lane >= DIFF_HD, q, zero)], axis=0)
    l_fin = l_scr[...]
    acc = None
    n_rb = 2 * tq // DIFF_ROW_BLOCK
    for ci, (seg, r0, c0, w) in enumerate(chunks):
        first, last = ci == 0, ci == len(chunks) - 1
        pv = _dot(e_scr[:, c0:c0 + w], v_refs[seg][0, r0:r0 + w, :])
        acc = pv if acc is None else acc + pv
        for rb in range(n_rb):
            rows = slice(rb * DIFF_ROW_BLOCK, (rb + 1) * DIFF_ROW_BLOCK)
            m_prev = m_scr[rows, :]
            l_run = None if first else la_scr[rows, :]
            for j in range(w // LANES):
                cols = slice(c0 + j * LANES, c0 + (j + 1) * LANES)
                ej = jnp.exp(s_scr[rows, cols] - m_prev)
                l_run = ej if l_run is None else l_run + ej
                e_scr[rows, cols] = ej.astype(BF16)
            if last:
                l_scr[rows, :] = jnp.broadcast_to(jnp.sum(l_run, axis=1, keepdims=True), l_run.shape)
            else:
                la_scr[rows, :] = l_run
        s_scr[:, c0:c0 + w] = _dot_nt(qs, k_refs[seg][0, r0:r0 + w, :])
        for rb in range(n_rb):
            rows = slice(rb * DIFF_ROW_BLOCK, (rb + 1) * DIFF_ROW_BLOCK)
            m_run = None if first else ma_scr[rows, :]
            for j in range(w // LANES):
                sj = s_scr[rows, c0 + j * LANES:c0 + (j + 1) * LANES]
                m_run = sj if m_run is None else jnp.maximum(m_run, sj)
            if last:
                m_scr[rows, :] = jnp.broadcast_to(jnp.max(m_run, axis=1, keepdims=True), m_run.shape)
            else:
                ma_scr[rows, :] = m_run

    lv = lam_ref[...]
    lam = (jnp.exp(jnp.sum(lv[0:1] * lv[1:2], axis=-1, keepdims=True))
           - jnp.exp(jnp.sum(lv[2:3] * lv[3:4], axis=-1, keepdims=True)) + lambda_init)
    o = acc / l_fin
    o = o[:tq] - lam * o[tq:]
    o = o * lax.rsqrt(jnp.mean(o * o, axis=-1, keepdims=True) + SUBLN_EPS)
    o_ref[0] = o * sub_ref[...] * (1.0 - lambda_init)


def _diff_attn_lat(lat16, ctx16, lam_vecs, subln, lambda_init, tq):
    b, s, _ = lat16.shape
    sc = ctx16.shape[1]
    nh = DIFF_HEADS
    nt = s // tq
    n_tiles = b * nh * nt

    def tile(g, lag):
        t = jnp.clip(g - lag, 0, n_tiles - 1)
        return t // (nh * nt), (t // nt) % nh, t % nt

    def q_map(g):
        bi, h, i = tile(g, 0)
        return (bi, i, h)

    def k_map(g):
        bi, h, _ = tile(g, 0)
        return (bi, 0, nh + h)

    def v_map(g):
        bi, h, _ = tile(g, 2)
        return (bi, 0, 2 * nh + h)

    def o_map(g):
        bi, h, i = tile(g, 2)
        return (bi, i, h)

    nk = sc + s
    stat = pltpu.VMEM((2 * tq, LANES), F32)
    return pl.pallas_call(
        functools.partial(_diff_pipe_kernel, lambda_init=lambda_init),
        grid=(n_tiles + 2,),
        in_specs=[
            pl.BlockSpec((1, tq, LANES), q_map),
            pl.BlockSpec((1, sc, LANES), k_map),
            pl.BlockSpec((1, s, LANES), k_map),
            pl.BlockSpec((1, sc, LANES), v_map),
            pl.BlockSpec((1, s, LANES), v_map),
            pl.BlockSpec((4, DIFF_HD), lambda g: (0, 0)),
            pl.BlockSpec((1, DIFF_VD), lambda g: (0, 0)),
        ],
        out_specs=pl.BlockSpec((1, tq, LANES), o_map),
        out_shape=jax.ShapeDtypeStruct((b, s, BRANCH_W), F32),
        scratch_shapes=[pltpu.VMEM((2 * tq, nk), F32), pltpu.VMEM((2 * tq, nk), BF16), stat, stat, stat, stat],
        compiler_params=_cparams("arbitrary"),
        name="diff_attn_lat",
    )(lat16, ctx16, lat16, ctx16, lat16, lam_vecs, subln)


NA_Q_BLK, NA_K_BLK, NA_V_BLK = 3, 4, 5


def _na_heads(q_ref, segs, o_ref):
    tq = q_ref.shape[1]
    lane = lax.broadcasted_iota(jnp.int32, (tq, LANES), 1)
    zero = jnp.zeros((tq, LANES), BF16)
    for hp in range(NA_HEADS // 2):
        q = q_ref[0, :, hp * LANES:(hp + 1) * LANES] * (NA_HD ** -0.5)
        qs = jnp.concatenate([jnp.where(lane < NA_HD, q, zero), jnp.where(lane >= NA_HD, q, zero)], axis=0)
        scores = []
        for k_of, _, bias_of in segs:
            sc = _dot_nt(qs, k_of(hp))
            if bias_of is not None:
                sc = sc + bias_of(hp)
            scores.append(sc)
        es, den = _softmax_parts(scores)
        o = None
        for (_, v_of, _), e in zip(segs, es):
            pv = _dot(e.astype(BF16), v_of(hp))
            o = pv if o is None else o + pv
        o = o / den
        o_ref[0, :, hp * LANES:(hp + 1) * LANES] = jnp.where(lane < NA_HD, o[:tq], o[tq:])


def _na_lat_kernel(q_ref, k_ref, v_ref, kc_ref, vc_ref, bias_ref, o_ref, *, rows):
    blk = pl.program_id(1)
    kstart = jnp.clip(blk * NA_ROWS_PER_BLOCK - NA_KH // 2, 0, rows - NA_KEY_ROWS)
    off = pl.multiple_of(kstart * GRID_W, GRID_W)
    nk = NA_KEY_ROWS * GRID_W
    pair = lambda hp: slice(hp * LANES, (hp + 1) * LANES)
    window = (lambda hp: k_ref[0, pl.ds(off, nk), pair(hp)],
              lambda hp: v_ref[0, pl.ds(off, nk), pair(hp)],
              lambda hp: jnp.concatenate([bias_ref[0, 2 * hp], bias_ref[0, 2 * hp + 1]], axis=0))
    context = (lambda hp: kc_ref[0, :, pair(hp)], lambda hp: vc_ref[0, :, pair(hp)], None)
    _na_heads(q_ref, [window, context], o_ref)


def _na_lat(lat16, ctx16, bias):
    b, s, _ = lat16.shape
    sc = ctx16.shape[1]
    rows = s // GRID_W
    nblk = rows // NA_ROWS_PER_BLOCK
    tq = NA_ROWS_PER_BLOCK * GRID_W
    nk = NA_KEY_ROWS * GRID_W
    bw = BRANCH_W

    def bias_map(bi, blk):
        case = jnp.where(blk == 0, 0, jnp.where(blk == nblk - 1, 2, 1))
        return (case, 0, 0, 0)

    return pl.pallas_call(
        functools.partial(_na_lat_kernel, rows=rows),
        grid=(b, nblk),
        in_specs=[
            pl.BlockSpec((1, tq, bw), lambda bi, blk: (bi, blk, NA_Q_BLK)),
            pl.BlockSpec((1, s, bw), lambda bi, blk: (bi, 0, NA_K_BLK)),
            pl.BlockSpec((1, s, bw), lambda bi, blk: (bi, 0, NA_V_BLK)),
            pl.BlockSpec((1, sc, bw), lambda bi, blk: (bi, 0, NA_K_BLK)),
            pl.BlockSpec((1, sc, bw), lambda bi, blk: (bi, 0, NA_V_BLK)),
            pl.BlockSpec((1, NA_HEADS, tq, nk), bias_map),
        ],
        out_specs=pl.BlockSpec((1, tq, bw), lambda bi, blk: (bi, blk, 0)),
        out_shape=jax.ShapeDtypeStruct((b, s, bw), F32),
        compiler_params=_cparams("arbitrary", "arbitrary"),
        name="na_lat",
    )(lat16, lat16, lat16, ctx16, ctx16, bias)


def _na_ctx_kernel(q_ref, k_ref, v_ref, o_ref):
    pair = lambda hp: slice(hp * LANES, (hp + 1) * LANES)
    _na_heads(q_ref, [(lambda hp: k_ref[0, :, pair(hp)], lambda hp: v_ref[0, :, pair(hp)], None)], o_ref)


def _na_ctx(ctx16):
    b, sc, _ = ctx16.shape
    bw = BRANCH_W
    return pl.pallas_call(
        _na_ctx_kernel,
        grid=(b,),
        in_specs=[
            pl.BlockSpec((1, sc, bw), lambda bi: (bi, 0, NA_Q_BLK)),
            pl.BlockSpec((1, sc, bw), lambda bi: (bi, 0, NA_K_BLK)),
            pl.BlockSpec((1, sc, bw), lambda bi: (bi, 0, NA_V_BLK)),
        ],
        out_specs=pl.BlockSpec((1, sc, bw), lambda bi: (bi, 0, 0)),
        out_shape=jax.ShapeDtypeStruct((b, sc, bw), F32),
        compiler_params=_cparams("arbitrary"),
        name="na_ctx",
    )(ctx16, ctx16, ctx16)


def _na_bias(tab, rows):
    w = GRID_W
    lead = tab.shape[:-1]
    left = w - NA_KW
    g = jnp.pad(tab, [(0, 0)] * len(lead) + [(left, 2 * w - left - (2 * NA_KW - 1))])
    g = jnp.broadcast_to(g[..., None, :], lead + (w, 2 * w)).reshape(lead + (2 * w * w,))
    e = g[..., :w * (2 * w - 1)].reshape(lead + (w, 2 * w - 1))[..., w - 1:]
    col = np.arange(w)
    cstart = np.clip(col - NA_KW // 2, 0, w - NA_KW)
    col_ok = (col[None, :] >= cstart[:, None]) & (col[None, :] < cstart[:, None] + NA_KW)
    e = jnp.where(col_ok, e, MASK_VALUE)
    masked = jnp.full(lead[:-1] + (w, w), MASK_VALUE, F32)
    r_blk = NA_ROWS_PER_BLOCK
    cases = []
    for r0, kstart in ((0, 0), (r_blk, r_blk - NA_KH // 2), (rows - r_blk, rows - NA_KEY_ROWS)):
        q_rows = []
        for qi in range(r_blk):
            r = r0 + qi
            rs = min(max(r - NA_KH // 2, 0), rows - NA_KH)
            blocks = []
            for ki in range(NA_KEY_ROWS):
                kr = kstart + ki
                blocks.append(e[..., kr - r + NA_KH - 1, :, :] if rs <= kr < rs + NA_KH else masked)
            q_rows.append(jnp.concatenate(blocks, axis=-1))
        cases.append(jnp.concatenate(q_rows, axis=-2))
    return jnp.stack(cases, axis=1)


def _final_kernel(x_ref, mod_ref, yf_ref, yd_ref, yp_ref, yn_ref, wg_ref, bg_ref, wb_ref, wo_ref,
                  g_ref, b_ref, o_ref, *, alpha):
    x = x_ref[0]
    xm = _modulated_ln(x, mod_ref).astype(BF16)
    bw = BRANCH_W
    d = D_MODEL
    acc = None
    for i, y_ref in enumerate((yf_ref, yd_ref, yp_ref, yn_ref)):
        zg = _dot(xm, wg_ref[:, i * bw:(i + 1) * bw]) + bg_ref[:, i * bw:(i + 1) * bw]
        gated = (y_ref[0] * (zg * _sigmoid(zg))).astype(BF16)
        proj = _dot(gated, wb_ref[i])
        c0 = N_BRANCH * bw + i * d
        zm = _dot(xm, wg_ref[:, c0:c0 + d]) + bg_ref[:, c0:c0 + d]
        term = _sigmoid(zm) * proj
        acc = term if acc is None else acc + term
    out = _dot(acc.astype(BF16), wo_ref[...])
    h = alpha * x + mod_ref[0, 2:3, :] * out
    mu = jnp.mean(h, axis=-1, keepdims=True)
    hc = h - mu
    var = jnp.mean(hc * hc, axis=-1, keepdims=True)
    o_ref[0] = hc * lax.rsqrt(var + LN_EPS) * g_ref[...] + b_ref[...]


def _final(x, mod, per_batch_mod, ys, wg, bg, wb, wo, ln_g, ln_b, alpha, tm):
    b, s, d = x.shape
    ng = wg.shape[1]
    mod_map = (lambda bi, i: (bi, 0, 0)) if per_batch_mod else (lambda bi, i: (0, 0, 0))
    tile = lambda w: pl.BlockSpec((1, tm, w), lambda bi, i: (bi, i, 0))
    const2 = lambda shape: pl.BlockSpec(shape, lambda bi, i: (0, 0), pipeline_mode=pl.Buffered(1))
    return pl.pallas_call(
        functools.partial(_final_kernel, alpha=alpha),
        grid=(b, s // tm),
        in_specs=[
            tile(d),
            pl.BlockSpec((1, 3, d), mod_map),
            tile(BRANCH_W), tile(BRANCH_W), tile(BRANCH_W), tile(BRANCH_W),
            const2((d, ng)),
            const2((1, ng)),
            pl.BlockSpec((N_BRANCH, BRANCH_W, d), lambda bi, i: (0, 0, 0), pipeline_mode=pl.Buffered(1)),
            const2((d, d)),
            const2((1, d)),
            const2((1, d)),
        ],
        out_specs=tile(d),
        out_shape=jax.ShapeDtypeStruct((b, s, d), F32),
        compiler_params=_cparams("arbitrary", "arbitrary"),
        name="gate_merge_out",
    )(x, mod, *ys, wg, bg, wb, wo, ln_g, ln_b)


def _rope_tables(s):
    t = jnp.arange(s)
    rows = (t // GRID_W).astype(F32)
    cols = (t % GRID_W).astype(F32)
    nf = DIFF_HD // 4
    inv = ROPE_BASE ** (-jnp.arange(nf, dtype=F32) / nf)
    ang = jnp.stack([rows[:, None] * inv, cols[:, None] * inv], axis=1)
    cos = jnp.cos(ang)
    sin = jnp.sin(ang)
    cos64 = jnp.concatenate([cos[:, 0], cos[:, 0], cos[:, 1], cos[:, 1]], axis=-1)
    sin64 = jnp.concatenate([-sin[:, 0], sin[:, 0], -sin[:, 1], sin[:, 1]], axis=-1)
    rep = LANES // DIFF_HD
    return jnp.tile(cos64, (1, rep)), jnp.tile(sin64, (1, rep))


def _dft_tables(n):
    idx = jnp.arange(n, dtype=jnp.int32)
    m = (idx[:, None] * idx[None, :]) % n
    ang = m.astype(F32) * (2.0 * math.pi / n)
    return jnp.cos(ang), jnp.sin(ang)


def _split_cols(w, sizes):
    out, acc = [], 0
    for sz in sizes:
        out.append(w[..., acc:acc + sz])
        acc += sz
    return out


def kernel(x, c, ctx, c_ctx, w_mod, b_mod, w_in, b_in, fnet_w, diff_lam, diff_subln, pool_w, pool_scale,
           na_bias, w_branch, w_out, ln_g, ln_b):
    bsz, s, d = x.shape
    s_ctx = ctx.shape[1]
    rows = s // GRID_W
    assert d == D_MODEL and s % (GRID_W * NA_ROWS_PER_BLOCK) == 0 and rows >= NA_KEY_ROWS
    alpha = (2.0 * DEPTH) ** 0.25

    z = _split_cols(w_in, (BRANCH_W,) * 12 + (N_BRANCH * D_MODEL,))
    zb = _split_cols(b_in, (BRANCH_W,) * 12 + (N_BRANCH * D_MODEL,))
    order_a = (0, 6, 2, 3, 4, 8, 9, 10)
    order_g = (1, 5, 7, 11, 12)
    wa = jnp.concatenate([z[i] for i in order_a], axis=-1).astype(BF16)
    ba = jnp.concatenate([zb[i] for i in order_a], axis=-1)
    wg = jnp.concatenate([z[i] for i in order_g], axis=-1).astype(BF16)
    bg = jnp.concatenate([zb[i] for i in order_g], axis=-1)
    wb16 = w_branch.astype(BF16)
    wo16 = w_out.astype(BF16)
    fw16 = fnet_w.astype(BF16)
    pw16 = pool_w.astype(BF16)

    rope_tabs = _rope_tables(s)
    fft_tabs = {n: _fft_tables(n) for n in {s, s_ctx}}
    cc, sc_ = _dft_tables(FNET_GW)
    ccsc = jnp.concatenate([cc, sc_], axis=0).astype(BF16)
    na_bias_all = _na_bias(na_bias, rows)

    r_pad = -(-(bsz + 1) // 8) * 8
    cc_all = jnp.zeros((r_pad, d), F32).at[:bsz].set(c).at[bsz].set(c_ctx)
    mod_all = _modulation(cc_all, w_mod, b_mod)

    tm_lat = min(512, s)
    tm_fin = min(512, s)
    tq_diff = min(256, s)

    h, hc = x, ctx
    for l in range(DEPTH):
        lambda_init = 0.8 - 0.6 * math.exp(-0.3 * l)
        need_ctx = l < DEPTH - 1
        mod_lat = mod_all[l, :bsz].reshape(bsz, 3, d)
        mod_ctx = mod_all[l, bsz:bsz + 1].reshape(1, 3, d)

        lat32, lat16 = _inproj(h, mod_lat, True, wa[l], ba[l][None], rope_tabs, tm_lat)
        ctx32, ctx16 = _inproj(hc, mod_ctx, False, wa[l], ba[l][None], None, s_ctx)

        lam = diff_lam[l]
        sub = diff_subln[l][None]
        psc = pool_scale[l][None]
        y_f = _fnet_fft(lat32, fft_tabs[s], ccsc, fw16[l])
        y_d = _diff_attn_lat(lat16, ctx16, lam, sub, lambda_init, tq_diff)
        y_p = _pool(lat32, pw16[l], psc)
        y_n = _na_lat(lat16, ctx16, na_bias_all[l])
        h_new = _final(h, mod_lat, True, (y_f, y_d, y_p, y_n), wg[l], bg[l][None], wb16[l], wo16[l],
                       ln_g[l][None], ln_b[l][None], alpha, tm_fin)
        if need_ctx:
            y_fc = _fnet_fft(ctx32, fft_tabs[s_ctx], ccsc, fw16[l])
            y_dc = _diff_attn(ctx16, [ctx16], lam, sub, lambda_init, s_ctx)
            y_pc = _pool(ctx32, pw16[l], psc)
            y_nc = _na_ctx(ctx16)
            hc = _final(hc, mod_ctx, False, (y_fc, y_dc, y_pc, y_nc), wg[l], bg[l][None], wb16[l], wo16[l],
                        ln_g[l][None], ln_b[l][None], alpha, s_ctx)
        h = h_new
    return h
```

```python
import functools
import math

import numpy as np
import jax
import jax.numpy as jnp
from jax import lax
from jax.experimental import pallas as pl
from jax.experimental.pallas import tpu as pltpu

D_MODEL = 1024
DEPTH = 4
GRID_W = 64
N_BRANCH = 4
BRANCH_W = 512
FNET_GROUPS = 4
FNET_GW = BRANCH_W // FNET_GROUPS
DIFF_HEADS = 4
DIFF_HD = 64
DIFF_VD = 2 * DIFF_HD
POOL_WINDOWS = (2, 4, 8, 16)
POOL_GW = BRANCH_W // len(POOL_WINDOWS)
NA_HEADS = 8
NA_HD = BRANCH_W // NA_HEADS
NA_KH = 8
NA_KW = 16
ROPE_BASE = 10000.0
LN_EPS = 1e-6
SUBLN_EPS = 1e-5
PROJ_W = 12 * BRANCH_W + N_BRANCH * D_MODEL

LANES = 128
POOL_PAD = 16
NA_ROWS_PER_BLOCK = 4
NA_KEY_ROWS = NA_ROWS_PER_BLOCK + NA_KH - 1
MASK_VALUE = -1e30
VMEM_LIMIT = 56 * 1024 * 1024

F32 = jnp.float32
BF16 = jnp.bfloat16


def _cparams(*sem):
    return pltpu.CompilerParams(dimension_semantics=sem, vmem_limit_bytes=VMEM_LIMIT)


def _dot(a, b):
    return jnp.dot(a, b, preferred_element_type=F32)


def _dot_nt(a, b):
    return lax.dot_general(a, b, (((1,), (1,)), ((), ())), preferred_element_type=F32)


def _sigmoid(x):
    return 1.0 / (1.0 + jnp.exp(-x))


def _modulated_ln(x, mod_ref):
    mu = jnp.mean(x, axis=-1, keepdims=True)
    xc = x - mu
    var = jnp.mean(xc * xc, axis=-1, keepdims=True)
    y = xc * lax.rsqrt(var + LN_EPS)
    return y * (1.0 + mod_ref[0, 1:2, :]) + mod_ref[0, 0:1, :]


def _mod_kernel(c_ref, w_ref, b_ref, o_ref):
    c = c_ref[...]
    s = (c * _sigmoid(c)).astype(BF16)
    o_ref[0] = _dot(s, w_ref[0].astype(BF16)) + b_ref[0]


def _modulation(cc, w_mod, b_mod):
    n_l, d, d3 = w_mod.shape
    r = cc.shape[0]
    tn = d
    return pl.pallas_call(
        _mod_kernel,
        grid=(n_l, d3 // tn),
        in_specs=[
            pl.BlockSpec((r, d), lambda l, j: (0, 0)),
            pl.BlockSpec((1, d, tn), lambda l, j: (l, 0, j)),
            pl.BlockSpec((1, 1, tn), lambda l, j: (l, 0, j)),
        ],
        out_specs=pl.BlockSpec((1, r, tn), lambda l, j: (l, 0, j)),
        out_shape=jax.ShapeDtypeStruct((n_l, r, d3), F32),
        compiler_params=_cparams("arbitrary", "arbitrary"),
        name="modulation",
    )(cc, w_mod, b_mod.reshape(n_l, 1, d3))


def _cast_kernel(w_ref, o_ref):
    o_ref[...] = w_ref[...].astype(BF16)


def _gather_cast_cols(w, col_blocks):
    n_l, d, _ = w.shape
    nb = len(col_blocks)

    def src_map(l, j):
        idx = jnp.int32(col_blocks[0])
        for k, cb in enumerate(col_blocks[1:], start=1):
            idx = jnp.where(j == k, cb, idx)
        return (l, 0, idx)

    return pl.pallas_call(
        _cast_kernel,
        grid=(n_l, nb),
        in_specs=[pl.BlockSpec((1, d, BRANCH_W), src_map)],
        out_specs=pl.BlockSpec((1, d, BRANCH_W), lambda l, j: (l, 0, j)),
        out_shape=jax.ShapeDtypeStruct((n_l, d, nb * BRANCH_W), BF16),
        compiler_params=_cparams("arbitrary", "arbitrary"),
        name="gather_cast_cols",
    )(w)


def _rope(r, cos_ref, sin_ref):
    lane = lax.broadcasted_iota(jnp.int32, (r.shape[0], LANES), 1)
    first = (lane % 32) < 16
    cos = cos_ref[...]
    sin = sin_ref[...]
    outs = []
    for k in range(r.shape[1] // LANES):
        xk = r[:, k * LANES:(k + 1) * LANES]
        partner = jnp.where(first, pltpu.roll(xk, LANES - 16, 1), pltpu.roll(xk, 16, 1))
        outs.append(xk * cos + partner * sin)
    return jnp.concatenate(outs, axis=1)


def _inproj_kernel(*refs, rope):
    if rope:
        x_ref, mod_ref, w_ref, b_ref, cos_ref, sin_ref, o32_ref, o16_ref = refs
    else:
        x_ref, mod_ref, w_ref, b_ref, o32_ref, o16_ref = refs
    xm = _modulated_ln(x_ref[0], mod_ref).astype(BF16)
    bw = BRANCH_W
    for j in range(2):
        o32_ref[0, :, j * bw:(j + 1) * bw] = _dot(xm, w_ref[:, j * bw:(j + 1) * bw]) + b_ref[:, j * bw:(j + 1) * bw]
    for j in range(6):
        c0 = (2 + j) * bw
        r = _dot(xm, w_ref[:, c0:c0 + bw]) + b_ref[:, c0:c0 + bw]
        if rope and j in (0, 1):
            r = _rope(r, cos_ref, sin_ref)
        o16_ref[0, :, j * bw:(j + 1) * bw] = r.astype(BF16)


def _inproj(x, mod, per_batch_mod, wa, ba, rope_tabs, tm):
    b, s, d = x.shape
    na = wa.shape[1]
    mod_map = (lambda bi, i: (bi, 0, 0)) if per_batch_mod else (lambda bi, i: (0, 0, 0))
    in_specs = [
        pl.BlockSpec((1, tm, d), lambda bi, i: (bi, i, 0)),
        pl.BlockSpec((1, 3, d), mod_map),
        pl.BlockSpec((d, na), lambda bi, i: (0, 0)),
        pl.BlockSpec((1, na), lambda bi, i: (0, 0)),
    ]
    args = [x, mod, wa, ba]
    if rope_tabs is not None:
        in_specs += [pl.BlockSpec((tm, LANES), lambda bi, i: (i, 0))] * 2
        args += list(rope_tabs)
    return pl.pallas_call(
        functools.partial(_inproj_kernel, rope=rope_tabs is not None),
        grid=(b, s // tm),
        in_specs=in_specs,
        out_specs=[
            pl.BlockSpec((1, tm, 2 * BRANCH_W), lambda bi, i: (bi, i, 0)),
            pl.BlockSpec((1, tm, 6 * BRANCH_W), lambda bi, i: (bi, i, 0)),
        ],
        out_shape=[
            jax.ShapeDtypeStruct((b, s, 2 * BRANCH_W), F32),
            jax.ShapeDtypeStruct((b, s, 6 * BRANCH_W), BF16),
        ],
        compiler_params=_cparams("arbitrary", "arbitrary"),
        name="inproj_rope" if rope_tabs is not None else "inproj",
    )(*args)


FFT_PAD = 4
FFT_GROUPS_PER_STEP = 2
FFT_UNROLL = 8


def _fft_kernel(u_ref, f1_ref, f2_ref, tw_ref, cc_ref, w_ref, o_ref, t_scr, z_scr, *, r, scale):
    p = r + FFT_PAD
    ng = FFT_GROUPS_PER_STEP
    lanes = lambda g: slice(g * LANES, (g + 1) * LANES)

    def transpose_in(n1, c):
        blk = u_ref[0, pl.ds(pl.multiple_of(n1 * r, r), r), :]
        for g in range(ng):
            t_scr[g, pl.ds(n1, r, stride=p), :] = blk[:, lanes(g)]
        return c

    lax.fori_loop(0, r, transpose_in, 0, unroll=FFT_UNROLL)

    def stage1(n2, c):
        m = jnp.concatenate([t_scr[g, pl.ds(n2 * p, r), :] for g in range(ng)], axis=1).astype(BF16)
        y = _dot(f1_ref[...], m)
        tr = tw_ref[n2, 0]
        ti = tw_ref[n2, 1]
        for g in range(ng):
            a = y[:r, lanes(g)]
            b = y[r:, lanes(g)]
            z_scr[0, g, pl.ds(n2, r, stride=p), :] = a * tr - b * ti
            z_scr[1, g, pl.ds(n2, r, stride=p), :] = a * ti + b * tr
        return c

    lax.fori_loop(0, r, stage1, 0, unroll=FFT_UNROLL)

    def stage2(blk, c):
        xs = []
        for i in range(FFT_UNROLL):
            k1 = blk * FFT_UNROLL + i
            zz = jnp.concatenate(
                [jnp.concatenate([z_scr[ri, g, pl.ds(k1 * p, r), :] for g in range(ng)], axis=1) for ri in range(2)],
                axis=0).astype(BF16)
            xs.append(_dot(f2_ref[...], zz))
        for g in range(ng):
            xg = jnp.concatenate([jnp.concatenate([x[:r, lanes(g)], x[r:, lanes(g)]], axis=1) for x in xs], axis=0)
            f = _dot(xg.astype(BF16), cc_ref[...]) * scale
            y = _dot(f.astype(BF16), w_ref[g])
            for i in range(FFT_UNROLL):
                t_scr[g, pl.ds(blk * FFT_UNROLL + i, r, stride=p), :] = y[i * r:(i + 1) * r]
        return c

    lax.fori_loop(0, r // FFT_UNROLL, stage2, 0)

    def copy_out(k2, c):
        for g in range(ng):
            o_ref[0, pl.ds(pl.multiple_of(k2 * r, r), r), lanes(g)] = t_scr[g, pl.ds(k2 * p, r), :]
        return c

    lax.fori_loop(0, r, copy_out, 0, unroll=FFT_UNROLL)


def _fft_tables(s):
    r = math.isqrt(s)
    assert r * r == s and r % FFT_UNROLL == 0
    cr, sr = _dft_tables(r)
    f1 = jnp.concatenate([cr, -sr], axis=0).astype(BF16)
    f2 = jnp.concatenate([jnp.concatenate([cr, sr], axis=1), jnp.concatenate([-sr, cr], axis=1)], axis=0).astype(BF16)
    idx = jnp.arange(r, dtype=jnp.int32)
    ang = (idx[:, None] * idx[None, :]).astype(F32) * (2.0 * math.pi / s)
    tw = jnp.stack([jnp.cos(ang), -jnp.sin(ang)], axis=1)
    tw = jnp.broadcast_to(tw[..., None], tw.shape + (LANES,))
    return f1, f2, tw


def _fnet_fft(o32, tabs, ccsc, w):
    b, s, _ = o32.shape
    f1, f2, tw = tabs
    r = f1.shape[1]
    p = r + FFT_PAD
    ng = FFT_GROUPS_PER_STEP
    scale = 1.0 / math.sqrt(s * FNET_GW)
    return pl.pallas_call(
        functools.partial(_fft_kernel, r=r, scale=scale),
        grid=(b, FNET_GROUPS // ng),
        in_specs=[
            pl.BlockSpec((1, s, ng * LANES), lambda bi, j: (bi, 0, j)),
            pl.BlockSpec((2 * r, r), lambda bi, j: (0, 0)),
            pl.BlockSpec((2 * r, 2 * r), lambda bi, j: (0, 0)),
            pl.BlockSpec((r, 2, r, LANES), lambda bi, j: (0, 0, 0, 0)),
            pl.BlockSpec((2 * FNET_GW, FNET_GW), lambda bi, j: (0, 0)),
            pl.BlockSpec((ng, FNET_GW, FNET_GW), lambda bi, j: (j, 0, 0)),
        ],
        out_specs=pl.BlockSpec((1, s, ng * LANES), lambda bi, j: (bi, 0, j)),
        out_shape=jax.ShapeDtypeStruct((b, s, BRANCH_W), F32),
        scratch_shapes=[pltpu.VMEM((ng, r * p, LANES), F32), pltpu.VMEM((2, ng, r * p, LANES), F32)],
        compiler_params=_cparams("arbitrary", "arbitrary"),
        name="fnet_fft",
    )(o32, f1, f2, tw, ccsc, w)


def _pool_kernel(u_ref, w_ref, sc_ref, o_ref, pad_ref):
    s = u_ref.shape[1]
    gw = POOL_GW
    t = lax.broadcasted_iota(jnp.int32, (s, 1), 0)
    zeros = jnp.zeros((POOL_PAD, gw), F32)
    for g, win in enumerate(POOL_WINDOWS):
        u = u_ref[0, :, g * gw:(g + 1) * gw]
        pad_ref[0:POOL_PAD, :] = zeros
        pad_ref[POOL_PAD + s:2 * POOL_PAD + s, :] = zeros
        pad_ref[POOL_PAD:POOL_PAD + s, :] = u
        acc = None
        for j in range(-(win // 2), win - win // 2):
            term = pad_ref[POOL_PAD + j:POOL_PAD + j + s, :]
            acc = term if acc is None else acc + term
        lo = jnp.clip(t - win // 2, 0, s)
        hi = jnp.clip(t - win // 2 + win, 0, s)
        cnt = (hi - lo).astype(F32)
        pooled = (acc / cnt - u).astype(BF16)
        y = _dot(pooled, w_ref[g])
        o_ref[0, :, g * gw:(g + 1) * gw] = y * sc_ref[:, g * gw:(g + 1) * gw]


def _pool(o32, w, scale):
    b, s, _ = o32.shape
    return pl.pallas_call(
        _pool_kernel,
        grid=(b,),
        in_specs=[
            pl.BlockSpec((1, s, BRANCH_W), lambda bi: (bi, 0, 1)),
            pl.BlockSpec((len(POOL_WINDOWS), POOL_GW, POOL_GW), lambda bi: (0, 0, 0)),
            pl.BlockSpec((1, BRANCH_W), lambda bi: (0, 0)),
        ],
        out_specs=pl.BlockSpec((1, s, BRANCH_W), lambda bi: (bi, 0, 0)),
        out_shape=jax.ShapeDtypeStruct((b, s, BRANCH_W), F32),
        scratch_shapes=[pltpu.VMEM((s + 2 * POOL_PAD, POOL_GW), F32)],
        compiler_params=_cparams("arbitrary"),
        name="pool",
    )(o32, w, scale)


def _softmax_parts(scores):
    m = functools.reduce(jnp.maximum, [jnp.max(s, axis=-1, keepdims=True) for s in scores])
    es = [jnp.exp(s - m) for s in scores]
    den = functools.reduce(lambda a, c: a + c, [jnp.sum(e, axis=-1, keepdims=True) for e in es])
    return es, den


def _diff_kernel(*refs, n_seg, lambda_init):
    q_ref = refs[0]
    k_refs = refs[1:1 + n_seg]
    v_refs = refs[1 + n_seg:1 + 2 * n_seg]
    lam_ref, sub_ref, o_ref = refs[1 + 2 * n_seg:]
    lv = lam_ref[...]
    lam = (jnp.exp(jnp.sum(lv[0:1] * lv[1:2], axis=-1, keepdims=True))
           - jnp.exp(jnp.sum(lv[2:3] * lv[3:4], axis=-1, keepdims=True)) + lambda_init)
    tq = q_ref.shape[1]
    lane = lax.broadcasted_iota(jnp.int32, (tq, LANES), 1)
    zero = jnp.zeros((tq, LANES), BF16)
    q = q_ref[0] * (DIFF_HD ** -0.5)
    qs = jnp.concatenate([jnp.where(lane < DIFF_HD, q, zero), jnp.where(lane >= DIFF_HD, q, zero)], axis=0)
    es, den = _softmax_parts([_dot_nt(qs, k_ref[0]) for k_ref in k_refs])
    l1 = den[:tq]
    rho = lam * l1 / den[tq:]
    o = None
    for seg in range(n_seg):
        a = (es[seg][:tq] - rho * es[seg][tq:]).astype(BF16)
        pv = _dot(a, v_refs[seg][0])
        o = pv if o is None else o + pv
    o = o / l1
    o = o * lax.rsqrt(jnp.mean(o * o, axis=-1, keepdims=True) + SUBLN_EPS)
    o_ref[0] = o * sub_ref[...] * (1.0 - lambda_init)


def _diff_attn(q16, kv16_list, lam_vecs, subln, lambda_init, tq):
    b, sq, _ = q16.shape
    n_seg = len(kv16_list)
    nh = DIFF_HEADS
    in_specs = [pl.BlockSpec((1, tq, LANES), lambda bi, h, i: (bi, i, h))]
    in_specs += [pl.BlockSpec((1, kv.shape[1], LANES), lambda bi, h, i: (bi, 0, nh + h)) for kv in kv16_list]
    in_specs += [pl.BlockSpec((1, kv.shape[1], LANES), lambda bi, h, i: (bi, 0, 2 * nh + h)) for kv in kv16_list]
    in_specs += [
        pl.BlockSpec((4, DIFF_HD), lambda bi, h, i: (0, 0)),
        pl.BlockSpec((1, DIFF_VD), lambda bi, h, i: (0, 0)),
    ]
    return pl.pallas_call(
        functools.partial(_diff_kernel, n_seg=n_seg, lambda_init=lambda_init),
        grid=(b, nh, sq // tq),
        in_specs=in_specs,
        out_specs=pl.BlockSpec((1, tq, LANES), lambda bi, h, i: (bi, i, h)),
        out_shape=jax.ShapeDtypeStruct((b, sq, BRANCH_W), F32),
        compiler_params=_cparams("arbitrary", "arbitrary", "arbitrary"),
        name="diff_attn",
    )(q16, *kv16_list, *kv16_list, lam_vecs, subln)


DIFF_KEY_CHUNK = 512
DIFF_ROW_BLOCK = 64


def _key_chunks(seg_lens):
    chunks, col = [], 0
    for seg, n in enumerate(seg_lens):
        for r0 in range(0, n, DIFF_KEY_CHUNK):
            w = min(DIFF_KEY_CHUNK, n - r0)
            chunks.append((seg, r0, col, w))
            col += w
    return chunks


def _diff_pipe_kernel(q_ref, kc_ref, kl_ref, vc_ref, vl_ref, lam_ref, sub_ref, o_ref,
                      s_scr, e_scr, m_scr, l_scr, ma_scr, la_scr, *, lambda_init):
    g = pl.program_id(0)
    tq = q_ref.shape[1]
    k_refs = (kc_ref, kl_ref)
    v_refs = (vc_ref, vl_ref)
    chunks = _key_chunks([r.shape[1] for r in k_refs])

    @pl.when(g == 0)
    def _():
        s_scr[...] = jnp.zeros(s_scr.shape, F32)
        e_scr[...] = jnp.zeros(e_scr.shape, BF16)
        m_scr[...] = jnp.zeros(m_scr.shape, F32)
        l_scr[...] = jnp.ones(l_scr.shape, F32)

    lane = lax.broadcasted_iota(jnp.int32, (tq, LANES), 1)
    zero = jnp.zeros((tq, LANES), BF16)
    q = q_ref[0] * (DIFF_HD ** -0.5)
    qs = jnp.concatenate([jnp.where(lane < DIFF_HD, q, zero), jnp.where(lane >= DIFF_HD, q, zero)], axis=0)
    l_fin = l_scr[...]
    acc = None
    n_rb = 2 * tq // DIFF_ROW_BLOCK
    for ci, (seg, r0, c0, w) in enumerate(chunks):
        first, last = ci == 0, ci == len(chunks) - 1
        pv = _dot(e_scr[:, c0:c0 + w], v_refs[seg][0, r0:r0 + w, :])
        acc = pv if acc is None else acc + pv
        for rb in range(n_rb):
            rows = slice(rb * DIFF_ROW_BLOCK, (rb + 1) * DIFF_ROW_BLOCK)
            m_prev = m_scr[rows, :]
            l_run = None if first else la_scr[rows, :]
            for j in range(w // LANES):
                cols = slice(c0 + j * LANES, c0 + (j + 1) * LANES)
                ej = jnp.exp(s_scr[rows, cols] - m_prev)
                l_run = ej if l_run is None else l_run + ej
                e_scr[rows, cols] = ej.astype(BF16)
            if last:
                l_scr[rows, :] = jnp.broadcast_to(jnp.sum(l_run, axis=1, keepdims=True), l_run.shape)
            else:
                la_scr[rows, :] = l_run
        s_scr[:, c0:c0 + w] = _dot_nt(qs, k_refs[seg][0, r0:r0 + w, :])
        for rb in range(n_rb):
            rows = slice(rb * DIFF_ROW_BLOCK, (rb + 1) * DIFF_ROW_BLOCK)
            m_run = None if first else ma_scr[rows, :]
            for j in range(w // LANES):
                sj = s_scr[rows, c0 + j * LANES:c0 + (j + 1) * LANES]
                m_run = sj if m_run is None else jnp.maximum(m_run, sj)
            if last:
                m_scr[rows, :] = jnp.broadcast_to(jnp.max(m_run, axis=1, keepdims=True), m_run.shape)
            else:
                ma_scr[rows, :] = m_run

    lv = lam_ref[...]
    lam = (jnp.exp(jnp.sum(lv[0:1] * lv[1:2], axis=-1, keepdims=True))
           - jnp.exp(jnp.sum(lv[2:3] * lv[3:4], axis=-1, keepdims=True)) + lambda_init)
    o = acc / l_fin
    o = o[:tq] - lam * o[tq:]
    o = o * lax.rsqrt(jnp.mean(o * o, axis=-1, keepdims=True) + SUBLN_EPS)
    o_ref[0] = o * sub_ref[...] * (1.0 - lambda_init)


def _diff_attn_lat(lat16, ctx16, lam_vecs, subln, lambda_init, tq):
    b, s, _ = lat16.shape
    sc = ctx16.shape[1]
    nh = DIFF_HEADS
    nt = s // tq
    n_tiles = b * nh * nt

    def tile(g, lag):
        t = jnp.clip(g - lag, 0, n_tiles - 1)
        return t // (nh * nt), (t // nt) % nh, t % nt

    def q_map(g):
        bi, h, i = tile(g, 0)
        return (bi, i, h)

    def k_map(g):
        bi, h, _ = tile(g, 0)
        return (bi, 0, nh + h)

    def v_map(g):
        bi, h, _ = tile(g, 2)
        return (bi, 0, 2 * nh + h)

    def o_map(g):
        bi, h, i = tile(g, 2)
        return (bi, i, h)

    nk = sc + s
    stat = pltpu.VMEM((2 * tq, LANES), F32)
    return pl.pallas_call(
        functools.partial(_diff_pipe_kernel, lambda_init=lambda_init),
        grid=(n_tiles + 2,),
        in_specs=[
            pl.BlockSpec((1, tq, LANES), q_map),
            pl.BlockSpec((1, sc, LANES), k_map),
            pl.BlockSpec((1, s, LANES), k_map),
            pl.BlockSpec((1, sc, LANES), v_map),
            pl.BlockSpec((1, s, LANES), v_map),
            pl.BlockSpec((4, DIFF_HD), lambda g: (0, 0)),
            pl.BlockSpec((1, DIFF_VD), lambda g: (0, 0)),
        ],
        out_specs=pl.BlockSpec((1, tq, LANES), o_map),
        out_shape=jax.ShapeDtypeStruct((b, s, BRANCH_W), F32),
        scratch_shapes=[pltpu.VMEM((2 * tq, nk), F32), pltpu.VMEM((2 * tq, nk), BF16), stat, stat, stat, stat],
        compiler_params=_cparams("arbitrary"),
        name="diff_attn_lat",
    )(lat16, ctx16, lat16, ctx16, lat16, lam_vecs, subln)


NA_Q_BLK, NA_K_BLK, NA_V_BLK = 3, 4, 5


def _na_heads(q_ref, segs, o_ref):
    tq = q_ref.shape[1]
    lane = lax.broadcasted_iota(jnp.int32, (tq, LANES), 1)
    zero = jnp.zeros((tq, LANES), BF16)
    for hp in range(NA_HEADS // 2):
        q = q_ref[0, :, hp * LANES:(hp + 1) * LANES] * (NA_HD ** -0.5)
        qs = jnp.concatenate([jnp.where(lane < NA_HD, q, zero), jnp.where(lane >= NA_HD, q, zero)], axis=0)
        scores = []
        for k_of, _, bias_of in segs:
            sc = _dot_nt(qs, k_of(hp))
            if bias_of is not None:
                sc = sc + bias_of(hp)
            scores.append(sc)
        es, den = _softmax_parts(scores)
        o = None
        for (_, v_of, _), e in zip(segs, es):
            pv = _dot(e.astype(BF16), v_of(hp))
            o = pv if o is None else o + pv
        o = o / den
        o_ref[0, :, hp * LANES:(hp + 1) * LANES] = jnp.where(lane < NA_HD, o[:tq], o[tq:])


def _na_lat_kernel(q_ref, k_ref, v_ref, kc_ref, vc_ref, bias_ref, o_ref, *, rows):
    blk = pl.program_id(1)
    kstart = jnp.clip(blk * NA_ROWS_PER_BLOCK - NA_KH // 2, 0, rows - NA_KEY_ROWS)
    off = pl.multiple_of(kstart * GRID_W, GRID_W)
    nk = NA_KEY_ROWS * GRID_W
    pair = lambda hp: slice(hp * LANES, (hp + 1) * LANES)
    window = (lambda hp: k_ref[0, pl.ds(off, nk), pair(hp)],
              lambda hp: v_ref[0, pl.ds(off, nk), pair(hp)],
              lambda hp: jnp.concatenate([bias_ref[0, 2 * hp], bias_ref[0, 2 * hp + 1]], axis=0))
    context = (lambda hp: kc_ref[0, :, pair(hp)], lambda hp: vc_ref[0, :, pair(hp)], None)
    _na_heads(q_ref, [window, context], o_ref)


def _na_lat(lat16, ctx16, bias):
    b, s, _ = lat16.shape
    sc = ctx16.shape[1]
    rows = s // GRID_W
    nblk = rows // NA_ROWS_PER_BLOCK
    tq = NA_ROWS_PER_BLOCK * GRID_W
    nk = NA_KEY_ROWS * GRID_W
    bw = BRANCH_W

    def bias_map(bi, blk):
        case = jnp.where(blk == 0, 0, jnp.where(blk == nblk - 1, 2, 1))
        return (case, 0, 0, 0)

    return pl.pallas_call(
        functools.partial(_na_lat_kernel, rows=rows),
        grid=(b, nblk),
        in_specs=[
            pl.BlockSpec((1, tq, bw), lambda bi, blk: (bi, blk, NA_Q_BLK)),
            pl.BlockSpec((1, s, bw), lambda bi, blk: (bi, 0, NA_K_BLK)),
            pl.BlockSpec((1, s, bw), lambda bi, blk: (bi, 0, NA_V_BLK)),
            pl.BlockSpec((1, sc, bw), lambda bi, blk: (bi, 0, NA_K_BLK)),
            pl.BlockSpec((1, sc, bw), lambda bi, blk: (bi, 0, NA_V_BLK)),
            pl.BlockSpec((1, NA_HEADS, tq, nk), bias_map),
        ],
        out_specs=pl.BlockSpec((1, tq, bw), lambda bi, blk: (bi, blk, 0)),
        out_shape=jax.ShapeDtypeStruct((b, s, bw), F32),
        compiler_params=_cparams("arbitrary", "arbitrary"),
        name="na_lat",
    )(lat16, lat16, lat16, ctx16, ctx16, bias)


def _na_ctx_kernel(q_ref, k_ref, v_ref, o_ref):
    pair = lambda hp: slice(hp * LANES, (hp + 1) * LANES)
    _na_heads(q_ref, [(lambda hp: k_ref[0, :, pair(hp)], lambda hp: v_ref[0, :, pair(hp)], None)], o_ref)


def _na_ctx(ctx16):
    b, sc, _ = ctx16.shape
    bw = BRANCH_W
    return pl.pallas_call(
        _na_ctx_kernel,
        grid=(b,),
        in_specs=[
            pl.BlockSpec((1, sc, bw), lambda bi: (bi, 0, NA_Q_BLK)),
            pl.BlockSpec((1, sc, bw), lambda bi: (bi, 0, NA_K_BLK)),
            pl.BlockSpec((1, sc, bw), lambda bi: (bi, 0, NA_V_BLK)),
        ],
        out_specs=pl.BlockSpec((1, sc, bw), lambda bi: (bi, 0, 0)),
        out_shape=jax.ShapeDtypeStruct((b, sc, bw), F32),
        compiler_params=_cparams("arbitrary"),
        name="na_ctx",
    )(ctx16, ctx16, ctx16)


NA_BIAS_ROWS = 2 * NA_KH - 1
NA_BIAS_COLS = 2 * NA_KW - 1


def _na_bias_kernel(tab_ref, o_ref, e_scr, *, rows):
    l, h, case = pl.program_id(0), pl.program_id(1), pl.program_id(2)
    w = GRID_W

    @pl.when(case == 0)
    def _():
        c = lax.broadcasted_iota(jnp.int32, (w, 2 * w), 0)
        kc = lax.broadcasted_iota(jnp.int32, (w, 2 * w), 1) % w
        cstart = jnp.clip(c - NA_KW // 2, 0, w - NA_KW)
        col_ok = (kc >= cstart) & (kc < cstart + NA_KW)
        d = kc - c + (NA_KW - 1)
        base = (l * NA_HEADS + h) * (NA_BIAS_ROWS * NA_BIAS_COLS)
        for ro in range(NA_BIAS_ROWS):
            e = jnp.full((w, 2 * w), MASK_VALUE, F32)
            for t in range(NA_BIAS_COLS):
                e = jnp.where(col_ok & (d == t), tab_ref[base + ro * NA_BIAS_COLS + t], e)
            e_scr[ro] = e

    r_blk = NA_ROWS_PER_BLOCK
    r0 = jnp.where(case == 0, 0, jnp.where(case == 1, r_blk, rows - r_blk))
    kstart = jnp.where(case == 0, 0, jnp.where(case == 1, r_blk - NA_KH // 2, rows - NA_KEY_ROWS))
    lane = lax.broadcasted_iota(jnp.int32, (w, 2 * w), 1)
    masked = jnp.full((w, 2 * w), MASK_VALUE, F32)
    n_pairs = (NA_KEY_ROWS + 1) // 2
    for qi in range(r_blk):
        r = r0 + qi
        rs = jnp.clip(r - NA_KH // 2, 0, rows - NA_KH)
        pieces = []
        for kp in range(n_pairs):
            halves = []
            for ki in (2 * kp, 2 * kp + 1):
                kr = kstart + ki
                valid = (kr >= rs) & (kr < rs + NA_KH) & (ki < NA_KEY_ROWS)
                ro = jnp.clip(kr - r + NA_KH - 1, 0, NA_BIAS_ROWS - 1)
                halves.append(jnp.where(valid, e_scr[ro], masked))
            pieces.append(jnp.where(lane < w, halves[0], halves[1]))
        strip = jnp.concatenate(pieces, axis=1)
        o_ref[0, 0, 0, qi * w:(qi + 1) * w, :] = strip[:, :NA_KEY_ROWS * w]


def _na_bias(tab, rows):
    n_l = tab.shape[0]
    tq = NA_ROWS_PER_BLOCK * GRID_W
    nk = NA_KEY_ROWS * GRID_W
    return pl.pallas_call(
        functools.partial(_na_bias_kernel, rows=rows),
        grid_spec=pltpu.PrefetchScalarGridSpec(
            num_scalar_prefetch=1,
            grid=(n_l, NA_HEADS, 3),
            in_specs=[],
            out_specs=pl.BlockSpec((1, 1, 1, tq, nk), lambda l, h, case, tab_ref: (l, case, h, 0, 0)),
            scratch_shapes=[pltpu.VMEM((NA_BIAS_ROWS, GRID_W, 2 * GRID_W), F32)],
        ),
        out_shape=jax.ShapeDtypeStruct((n_l, 3, NA_HEADS, tq, nk), F32),
        compiler_params=_cparams("arbitrary", "arbitrary", "arbitrary"),
        name="na_bias",
    )(tab.reshape(-1))


def _final_kernel(x_ref, mod_ref, yf_ref, yd_ref, yp_ref, yn_ref, wg_ref, bg_ref, wb_ref, wo_ref,
                  g_ref, b_ref, o_ref, *, alpha):
    x = x_ref[0]
    xm = _modulated_ln(x, mod_ref).astype(BF16)
    bw = BRANCH_W
    d = D_MODEL
    acc = None
    for i, y_ref in enumerate((yf_ref, yd_ref, yp_ref, yn_ref)):
        zg = _dot(xm, wg_ref[:, i * bw:(i + 1) * bw]) + bg_ref[:, i * bw:(i + 1) * bw]
        gated = (y_ref[0] * (zg * _sigmoid(zg))).astype(BF16)
        proj = _dot(gated, wb_ref[i])
        c0 = N_BRANCH * bw + i * d
        zm = _dot(xm, wg_ref[:, c0:c0 + d]) + bg_ref[:, c0:c0 + d]
        term = _sigmoid(zm) * proj
        acc = term if acc is None else acc + term
    out = _dot(acc.astype(BF16), wo_ref[...])
    h = alpha * x + mod_ref[0, 2:3, :] * out
    mu = jnp.mean(h, axis=-1, keepdims=True)
    hc = h - mu
    var = jnp.mean(hc * hc, axis=-1, keepdims=True)
    o_ref[0] = hc * lax.rsqrt(var + LN_EPS) * g_ref[...] + b_ref[...]


def _final(x, mod, per_batch_mod, ys, wg, bg, wb, wo, ln_g, ln_b, alpha, tm):
    b, s, d = x.shape
    ng = wg.shape[1]
    mod_map = (lambda bi, i: (bi, 0, 0)) if per_batch_mod else (lambda bi, i: (0, 0, 0))
    tile = lambda w: pl.BlockSpec((1, tm, w), lambda bi, i: (bi, i, 0))
    const2 = lambda shape: pl.BlockSpec(shape, lambda bi, i: (0, 0), pipeline_mode=pl.Buffered(1))
    return pl.pallas_call(
        functools.partial(_final_kernel, alpha=alpha),
        grid=(b, s // tm),
        in_specs=[
            tile(d),
            pl.BlockSpec((1, 3, d), mod_map),
            tile(BRANCH_W), tile(BRANCH_W), tile(BRANCH_W), tile(BRANCH_W),
            const2((d, ng)),
            const2((1, ng)),
            pl.BlockSpec((N_BRANCH, BRANCH_W, d), lambda bi, i: (0, 0, 0), pipeline_mode=pl.Buffered(1)),
            const2((d, d)),
            const2((1, d)),
            const2((1, d)),
        ],
        out_specs=tile(d),
        out_shape=jax.ShapeDtypeStruct((b, s, d), F32),
        compiler_params=_cparams("arbitrary", "arbitrary"),
        name="gate_merge_out",
    )(x, mod, *ys, wg, bg, wb, wo, ln_g, ln_b)


def _rope_tables(s):
    t = jnp.arange(s)
    rows = (t // GRID_W).astype(F32)
    cols = (t % GRID_W).astype(F32)
    nf = DIFF_HD // 4
    inv = ROPE_BASE ** (-jnp.arange(nf, dtype=F32) / nf)
    ang = jnp.stack([rows[:, None] * inv, cols[:, None] * inv], axis=1)
    cos = jnp.cos(ang)
    sin = jnp.sin(ang)
    cos64 = jnp.concatenate([cos[:, 0], cos[:, 0], cos[:, 1], cos[:, 1]], axis=-1)
    sin64 = jnp.concatenate([-sin[:, 0], sin[:, 0], -sin[:, 1], sin[:, 1]], axis=-1)
    rep = LANES // DIFF_HD
    return jnp.tile(cos64, (1, rep)), jnp.tile(sin64, (1, rep))


def _dft_tables(n):
    idx = jnp.arange(n, dtype=jnp.int32)
    m = (idx[:, None] * idx[None, :]) % n
    ang = m.astype(F32) * (2.0 * math.pi / n)
    return jnp.cos(ang), jnp.sin(ang)


def _split_cols(w, sizes):
    out, acc = [], 0
    for sz in sizes:
        out.append(w[..., acc:acc + sz])
        acc += sz
    return out


def kernel(x, c, ctx, c_ctx, w_mod, b_mod, w_in, b_in, fnet_w, diff_lam, diff_subln, pool_w, pool_scale,
           na_bias, w_branch, w_out, ln_g, ln_b):
    bsz, s, d = x.shape
    s_ctx = ctx.shape[1]
    rows = s // GRID_W
    assert d == D_MODEL and s % (GRID_W * NA_ROWS_PER_BLOCK) == 0 and rows >= NA_KEY_ROWS
    alpha = (2.0 * DEPTH) ** 0.25

    blocks_a = (0, 6, 2, 3, 4, 8, 9, 10)
    blocks_g = (1, 5, 7, 11) + tuple(range(12, 12 + N_BRANCH * D_MODEL // BRANCH_W))
    wa = _gather_cast_cols(w_in, blocks_a)
    wg = _gather_cast_cols(w_in, blocks_g)
    zb = _split_cols(b_in, (BRANCH_W,) * (PROJ_W // BRANCH_W))
    ba = jnp.concatenate([zb[i] for i in blocks_a], axis=-1)
    bg = jnp.concatenate([zb[i] for i in blocks_g], axis=-1)
    wb16 = w_branch.astype(BF16)
    wo16 = w_out.astype(BF16)
    fw16 = fnet_w.astype(BF16)
    pw16 = pool_w.astype(BF16)

    rope_tabs = _rope_tables(s)
    fft_tabs = {n: _fft_tables(n) for n in {s, s_ctx}}
    cc, sc_ = _dft_tables(FNET_GW)
    ccsc = jnp.concatenate([cc, sc_], axis=0).astype(BF16)
    na_bias_all = _na_bias(na_bias, rows)

    r_pad = -(-(bsz + 1) // 8) * 8
    cc_all = jnp.zeros((r_pad, d), F32).at[:bsz].set(c).at[bsz].set(c_ctx)
    mod_all = _modulation(cc_all, w_mod, b_mod)

    tm_lat = min(512, s)
    tm_fin = min(512, s)
    tq_diff = min(256, s)

    h, hc = x, ctx
    for l in range(DEPTH):
        lambda_init = 0.8 - 0.6 * math.exp(-0.3 * l)
        need_ctx = l < DEPTH - 1
        mod_lat = mod_all[l, :bsz].reshape(bsz, 3, d)
        mod_ctx = mod_all[l, bsz:bsz + 1].reshape(1, 3, d)

        lat32, lat16 = _inproj(h, mod_lat, True, wa[l], ba[l][None], rope_tabs, tm_lat)
        ctx32, ctx16 = _inproj(hc, mod_ctx, False, wa[l], ba[l][None], None, s_ctx)

        lam = diff_lam[l]
        sub = diff_subln[l][None]
        psc = pool_scale[l][None]
        y_f = _fnet_fft(lat32, fft_tabs[s], ccsc, fw16[l])
        y_d = _diff_attn_lat(lat16, ctx16, lam, sub, lambda_init, tq_diff)
        y_p = _pool(lat32, pw16[l], psc)
        y_n = _na_lat(lat16, ctx16, na_bias_all[l])
        h_new = _final(h, mod_lat, True, (y_f, y_d, y_p, y_n), wg[l], bg[l][None], wb16[l], wo16[l],
                       ln_g[l][None], ln_b[l][None], alpha, tm_fin)
        if need_ctx:
            y_fc = _fnet_fft(ctx32, fft_tabs[s_ctx], ccsc, fw16[l])
            y_dc = _diff_attn(ctx16, [ctx16], lam, sub, lambda_init, s_ctx)
            y_pc = _pool(ctx32, pw16[l], psc)
            y_nc = _na_ctx(ctx16)
            hc = _final(hc, mod_ctx, False, (y_fc, y_dc, y_pc, y_nc), wg[l], bg[l][None], wb16[l], wo16[l],
                        ln_g[l][None], ln_b[l][None], alpha, s_ctx)
        h = h_new
    return h
```

```python
import functools
import math

import numpy as np
import jax
import jax.numpy as jnp
from jax import lax
from jax.experimental import pallas as pl
from jax.experimental.pallas import tpu as pltpu

D_MODEL = 1024
DEPTH = 4
GRID_W = 64
N_BRANCH = 4
BRANCH_W = 512
FNET_GROUPS = 4
FNET_GW = BRANCH_W // FNET_GROUPS
DIFF_HEADS = 4
DIFF_HD = 64
DIFF_VD = 2 * DIFF_HD
POOL_WINDOWS = (2, 4, 8, 16)
POOL_GW = BRANCH_W // len(POOL_WINDOWS)
NA_HEADS = 8
NA_HD = BRANCH_W // NA_HEADS
NA_KH = 8
NA_KW = 16
ROPE_BASE = 10000.0
LN_EPS = 1e-6
SUBLN_EPS = 1e-5
PROJ_W = 12 * BRANCH_W + N_BRANCH * D_MODEL

LANES = 128
POOL_PAD = 16
NA_ROWS_PER_BLOCK = 4
NA_KEY_ROWS = NA_ROWS_PER_BLOCK + NA_KH - 1
MASK_VALUE = -1e30
VMEM_LIMIT = 56 * 1024 * 1024

F32 = jnp.float32
BF16 = jnp.bfloat16


def _cparams(*sem):
    return pltpu.CompilerParams(dimension_semantics=sem, vmem_limit_bytes=VMEM_LIMIT)


def _dot(a, b):
    return jnp.dot(a, b, preferred_element_type=F32)


def _dot_nt(a, b):
    return lax.dot_general(a, b, (((1,), (1,)), ((), ())), preferred_element_type=F32)


def _sigmoid(x):
    return 1.0 / (1.0 + jnp.exp(-x))


def _modulated_ln(x, mod_ref):
    mu = jnp.mean(x, axis=-1, keepdims=True)
    xc = x - mu
    var = jnp.mean(xc * xc, axis=-1, keepdims=True)
    y = xc * lax.rsqrt(var + LN_EPS)
    return y * (1.0 + mod_ref[0, 1:2, :]) + mod_ref[0, 0:1, :]


def _mod_kernel(c_ref, w_ref, b_ref, o_ref):
    c = c_ref[...]
    s = (c * _sigmoid(c)).astype(BF16)
    o_ref[0] = _dot(s, w_ref[0].astype(BF16)) + b_ref[0]


def _modulation(cc, w_mod, b_mod):
    n_l, d, d3 = w_mod.shape
    r = cc.shape[0]
    tn = d
    return pl.pallas_call(
        _mod_kernel,
        grid=(n_l, d3 // tn),
        in_specs=[
            pl.BlockSpec((r, d), lambda l, j: (0, 0)),
            pl.BlockSpec((1, d, tn), lambda l, j: (l, 0, j)),
            pl.BlockSpec((1, 1, tn), lambda l, j: (l, 0, j)),
        ],
        out_specs=pl.BlockSpec((1, r, tn), lambda l, j: (l, 0, j)),
        out_shape=jax.ShapeDtypeStruct((n_l, r, d3), F32),
        compiler_params=_cparams("arbitrary", "arbitrary"),
        name="modulation",
    )(cc, w_mod, b_mod.reshape(n_l, 1, d3))


def _cast_kernel(w_ref, o_ref):
    o_ref[...] = w_ref[...].astype(BF16)


def _gather_cast_cols(w, col_blocks):
    n_l, d, _ = w.shape
    nb = len(col_blocks)

    def src_map(l, j):
        idx = jnp.int32(col_blocks[0])
        for k, cb in enumerate(col_blocks[1:], start=1):
            idx = jnp.where(j == k, cb, idx)
        return (l, 0, idx)

    return pl.pallas_call(
        _cast_kernel,
        grid=(n_l, nb),
        in_specs=[pl.BlockSpec((1, d, BRANCH_W), src_map)],
        out_specs=pl.BlockSpec((1, d, BRANCH_W), lambda l, j: (l, 0, j)),
        out_shape=jax.ShapeDtypeStruct((n_l, d, nb * BRANCH_W), BF16),
        compiler_params=_cparams("arbitrary", "arbitrary"),
        name="gather_cast_cols",
    )(w)


def _rope(r, cos_ref, sin_ref):
    lane = lax.broadcasted_iota(jnp.int32, (r.shape[0], LANES), 1)
    first = (lane % 32) < 16
    cos = cos_ref[...]
    sin = sin_ref[...]
    outs = []
    for k in range(r.shape[1] // LANES):
        xk = r[:, k * LANES:(k + 1) * LANES]
        partner = jnp.where(first, pltpu.roll(xk, LANES - 16, 1), pltpu.roll(xk, 16, 1))
        outs.append(xk * cos + partner * sin)
    return jnp.concatenate(outs, axis=1)


def _inproj_kernel(*refs, rope):
    if rope:
        x_ref, mod_ref, w_ref, b_ref, cos_ref, sin_ref, o32_ref, o16_ref = refs
    else:
        x_ref, mod_ref, w_ref, b_ref, o32_ref, o16_ref = refs
    xm = _modulated_ln(x_ref[0], mod_ref).astype(BF16)
    bw = BRANCH_W
    for j in range(2):
        o32_ref[0, :, j * bw:(j + 1) * bw] = _dot(xm, w_ref[:, j * bw:(j + 1) * bw]) + b_ref[:, j * bw:(j + 1) * bw]
    for j in range(6):
        c0 = (2 + j) * bw
        r = _dot(xm, w_ref[:, c0:c0 + bw]) + b_ref[:, c0:c0 + bw]
        if rope and j in (0, 1):
            r = _rope(r, cos_ref, sin_ref)
        o16_ref[0, :, j * bw:(j + 1) * bw] = r.astype(BF16)


def _inproj(x, mod, per_batch_mod, wa, layer, ba, rope_tabs, tm):
    b, s, d = x.shape
    na = wa.shape[2]
    mod_map = (lambda bi, i: (bi, 0, 0)) if per_batch_mod else (lambda bi, i: (0, 0, 0))
    in_specs = [
        pl.BlockSpec((1, tm, d), lambda bi, i: (bi, i, 0)),
        pl.BlockSpec((1, 3, d), mod_map),
        pl.BlockSpec((None, d, na), lambda bi, i: (layer, 0, 0)),
        pl.BlockSpec((1, na), lambda bi, i: (0, 0)),
    ]
    args = [x, mod, wa, ba]
    if rope_tabs is not None:
        in_specs += [pl.BlockSpec((tm, LANES), lambda bi, i: (i, 0))] * 2
        args += list(rope_tabs)
    return pl.pallas_call(
        functools.partial(_inproj_kernel, rope=rope_tabs is not None),
        grid=(b, s // tm),
        in_specs=in_specs,
        out_specs=[
            pl.BlockSpec((1, tm, 2 * BRANCH_W), lambda bi, i: (bi, i, 0)),
            pl.BlockSpec((1, tm, 6 * BRANCH_W), lambda bi, i: (bi, i, 0)),
        ],
        out_shape=[
            jax.ShapeDtypeStruct((b, s, 2 * BRANCH_W), F32),
            jax.ShapeDtypeStruct((b, s, 6 * BRANCH_W), BF16),
        ],
        compiler_params=_cparams("arbitrary", "arbitrary"),
        name="inproj_rope" if rope_tabs is not None else "inproj",
    )(*args)


FFT_PAD = 4
FFT_GROUPS_PER_STEP = 2
FFT_UNROLL = 8


def _fft_kernel(u_ref, f1_ref, f2_ref, tw_ref, cc_ref, w_ref, o_ref, t_scr, z_scr, *, r, scale):
    p = r + FFT_PAD
    ng = FFT_GROUPS_PER_STEP
    lanes = lambda g: slice(g * LANES, (g + 1) * LANES)

    def transpose_in(n1, c):
        blk = u_ref[0, pl.ds(pl.multiple_of(n1 * r, r), r), :]
        for g in range(ng):
            t_scr[g, pl.ds(n1, r, stride=p), :] = blk[:, lanes(g)]
        return c

    lax.fori_loop(0, r, transpose_in, 0, unroll=FFT_UNROLL)

    def stage1(n2, c):
        m = jnp.concatenate([t_scr[g, pl.ds(n2 * p, r), :] for g in range(ng)], axis=1).astype(BF16)
        y = _dot(f1_ref[...], m)
        tr = tw_ref[n2, 0]
        ti = tw_ref[n2, 1]
        for g in range(ng):
            a = y[:r, lanes(g)]
            b = y[r:, lanes(g)]
            z_scr[0, g, pl.ds(n2, r, stride=p), :] = a * tr - b * ti
            z_scr[1, g, pl.ds(n2, r, stride=p), :] = a * ti + b * tr
        return c

    lax.fori_loop(0, r, stage1, 0, unroll=FFT_UNROLL)

    def stage2(blk, c):
        xs = []
        for i in range(FFT_UNROLL):
            k1 = blk * FFT_UNROLL + i
            zz = jnp.concatenate(
                [jnp.concatenate([z_scr[ri, g, pl.ds(k1 * p, r), :] for g in range(ng)], axis=1) for ri in range(2)],
                axis=0).astype(BF16)
            xs.append(_dot(f2_ref[...], zz))
        for g in range(ng):
            xg = jnp.concatenate([jnp.concatenate([x[:r, lanes(g)], x[r:, lanes(g)]], axis=1) for x in xs], axis=0)
            f = _dot(xg.astype(BF16), cc_ref[...]) * scale
            y = _dot(f.astype(BF16), w_ref[g])
            for i in range(FFT_UNROLL):
                t_scr[g, pl.ds(blk * FFT_UNROLL + i, r, stride=p), :] = y[i * r:(i + 1) * r]
        return c

    lax.fori_loop(0, r // FFT_UNROLL, stage2, 0)

    def copy_out(k2, c):
        for g in range(ng):
            o_ref[0, pl.ds(pl.multiple_of(k2 * r, r), r), lanes(g)] = t_scr[g, pl.ds(k2 * p, r), :]
        return c

    lax.fori_loop(0, r, copy_out, 0, unroll=FFT_UNROLL)


def _fft_tables(s):
    r = math.isqrt(s)
    assert r * r == s and r % FFT_UNROLL == 0
    cr, sr = _dft_tables(r)
    f1 = jnp.concatenate([cr, -sr], axis=0).astype(BF16)
    f2 = jnp.concatenate([jnp.concatenate([cr, sr], axis=1), jnp.concatenate([-sr, cr], axis=1)], axis=0).astype(BF16)
    idx = jnp.arange(r, dtype=jnp.int32)
    ang = (idx[:, None] * idx[None, :]).astype(F32) * (2.0 * math.pi / s)
    tw = jnp.stack([jnp.cos(ang), -jnp.sin(ang)], axis=1)
    tw = jnp.broadcast_to(tw[..., None], tw.shape + (LANES,))
    return f1, f2, tw


def _fnet_fft(o32, tabs, ccsc, w):
    b, s, _ = o32.shape
    f1, f2, tw = tabs
    r = f1.shape[1]
    p = r + FFT_PAD
    ng = FFT_GROUPS_PER_STEP
    scale = 1.0 / math.sqrt(s * FNET_GW)
    return pl.pallas_call(
        functools.partial(_fft_kernel, r=r, scale=scale),
        grid=(b, FNET_GROUPS // ng),
        in_specs=[
            pl.BlockSpec((1, s, ng * LANES), lambda bi, j: (bi, 0, j)),
            pl.BlockSpec((2 * r, r), lambda bi, j: (0, 0)),
            pl.BlockSpec((2 * r, 2 * r), lambda bi, j: (0, 0)),
            pl.BlockSpec((r, 2, r, LANES), lambda bi, j: (0, 0, 0, 0)),
            pl.BlockSpec((2 * FNET_GW, FNET_GW), lambda bi, j: (0, 0)),
            pl.BlockSpec((ng, FNET_GW, FNET_GW), lambda bi, j: (j, 0, 0)),
        ],
        out_specs=pl.BlockSpec((1, s, ng * LANES), lambda bi, j: (bi, 0, j)),
        out_shape=jax.ShapeDtypeStruct((b, s, BRANCH_W), F32),
        scratch_shapes=[pltpu.VMEM((ng, r * p, LANES), F32), pltpu.VMEM((2, ng, r * p, LANES), F32)],
        compiler_params=_cparams("arbitrary", "arbitrary"),
        name="fnet_fft",
    )(o32, f1, f2, tw, ccsc, w)


def _pool_kernel(u_ref, w_ref, sc_ref, o_ref, pad_ref):
    s = u_ref.shape[1]
    gw = POOL_GW
    t = lax.broadcasted_iota(jnp.int32, (s, 1), 0)
    zeros = jnp.zeros((POOL_PAD, gw), F32)
    for g, win in enumerate(POOL_WINDOWS):
        u = u_ref[0, :, g * gw:(g + 1) * gw]
        pad_ref[0:POOL_PAD, :] = zeros
        pad_ref[POOL_PAD + s:2 * POOL_PAD + s, :] = zeros
        pad_ref[POOL_PAD:POOL_PAD + s, :] = u
        acc = None
        for j in range(-(win // 2), win - win // 2):
            term = pad_ref[POOL_PAD + j:POOL_PAD + j + s, :]
            acc = term if acc is None else acc + term
        lo = jnp.clip(t - win // 2, 0, s)
        hi = jnp.clip(t - win // 2 + win, 0, s)
        cnt = (hi - lo).astype(F32)
        pooled = (acc / cnt - u).astype(BF16)
        y = _dot(pooled, w_ref[g])
        o_ref[0, :, g * gw:(g + 1) * gw] = y * sc_ref[:, g * gw:(g + 1) * gw]


def _pool(o32, w, scale):
    b, s, _ = o32.shape
    return pl.pallas_call(
        _pool_kernel,
        grid=(b,),
        in_specs=[
            pl.BlockSpec((1, s, BRANCH_W), lambda bi: (bi, 0, 1)),
            pl.BlockSpec((len(POOL_WINDOWS), POOL_GW, POOL_GW), lambda bi: (0, 0, 0)),
            pl.BlockSpec((1, BRANCH_W), lambda bi: (0, 0)),
        ],
        out_specs=pl.BlockSpec((1, s, BRANCH_W), lambda bi: (bi, 0, 0)),
        out_shape=jax.ShapeDtypeStruct((b, s, BRANCH_W), F32),
        scratch_shapes=[pltpu.VMEM((s + 2 * POOL_PAD, POOL_GW), F32)],
        compiler_params=_cparams("arbitrary"),
        name="pool",
    )(o32, w, scale)


def _softmax_parts(scores):
    m = functools.reduce(jnp.maximum, [jnp.max(s, axis=-1, keepdims=True) for s in scores])
    es = [jnp.exp(s - m) for s in scores]
    den = functools.reduce(lambda a, c: a + c, [jnp.sum(e, axis=-1, keepdims=True) for e in es])
    return es, den


def _diff_kernel(*refs, n_seg, lambda_init):
    q_ref = refs[0]
    k_refs = refs[1:1 + n_seg]
    v_refs = refs[1 + n_seg:1 + 2 * n_seg]
    lam_ref, sub_ref, o_ref = refs[1 + 2 * n_seg:]
    lv = lam_ref[...]
    lam = (jnp.exp(jnp.sum(lv[0:1] * lv[1:2], axis=-1, keepdims=True))
           - jnp.exp(jnp.sum(lv[2:3] * lv[3:4], axis=-1, keepdims=True)) + lambda_init)
    tq = q_ref.shape[1]
    lane = lax.broadcasted_iota(jnp.int32, (tq, LANES), 1)
    zero = jnp.zeros((tq, LANES), BF16)
    q = q_ref[0] * (DIFF_HD ** -0.5)
    qs = jnp.concatenate([jnp.where(lane < DIFF_HD, q, zero), jnp.where(lane >= DIFF_HD, q, zero)], axis=0)
    es, den = _softmax_parts([_dot_nt(qs, k_ref[0]) for k_ref in k_refs])
    l1 = den[:tq]
    rho = lam * l1 / den[tq:]
    o = None
    for seg in range(n_seg):
        a = (es[seg][:tq] - rho * es[seg][tq:]).astype(BF16)
        pv = _dot(a, v_refs[seg][0])
        o = pv if o is None else o + pv
    o = o / l1
    o = o * lax.rsqrt(jnp.mean(o * o, axis=-1, keepdims=True) + SUBLN_EPS)
    o_ref[0] = o * sub_ref[...] * (1.0 - lambda_init)


def _diff_attn(q16, kv16_list, lam_vecs, subln, lambda_init, tq):
    b, sq, _ = q16.shape
    n_seg = len(kv16_list)
    nh = DIFF_HEADS
    in_specs = [pl.BlockSpec((1, tq, LANES), lambda bi, h, i: (bi, i, h))]
    in_specs += [pl.BlockSpec((1, kv.shape[1], LANES), lambda bi, h, i: (bi, 0, nh + h)) for kv in kv16_list]
    in_specs += [pl.BlockSpec((1, kv.shape[1], LANES), lambda bi, h, i: (bi, 0, 2 * nh + h)) for kv in kv16_list]
    in_specs += [
        pl.BlockSpec((4, DIFF_HD), lambda bi, h, i: (0, 0)),
        pl.BlockSpec((1, DIFF_VD), lambda bi, h, i: (0, 0)),
    ]
    return pl.pallas_call(
        functools.partial(_diff_kernel, n_seg=n_seg, lambda_init=lambda_init),
        grid=(b, nh, sq // tq),
        in_specs=in_specs,
        out_specs=pl.BlockSpec((1, tq, LANES), lambda bi, h, i: (bi, i, h)),
        out_shape=jax.ShapeDtypeStruct((b, sq, BRANCH_W), F32),
        compiler_params=_cparams("arbitrary", "arbitrary", "arbitrary"),
        name="diff_attn",
    )(q16, *kv16_list, *kv16_list, lam_vecs, subln)


DIFF_KEY_CHUNK = 512
DIFF_ROW_BLOCK = 64


def _key_chunks(seg_lens):
    chunks, col = [], 0
    for seg, n in enumerate(seg_lens):
        for r0 in range(0, n, DIFF_KEY_CHUNK):
            w = min(DIFF_KEY_CHUNK, n - r0)
            chunks.append((seg, r0, col, w))
            col += w
    return chunks


def _diff_pipe_kernel(q_ref, kc_ref, kl_ref, vc_ref, vl_ref, lam_ref, sub_ref, o_ref,
                      s_scr, e_scr, m_scr, l_scr, ma_scr, la_scr, *, lambda_init):
    g = pl.program_id(0)
    tq = q_ref.shape[1]
    k_refs = (kc_ref, kl_ref)
    v_refs = (vc_ref, vl_ref)
    chunks = _key_chunks([r.shape[1] for r in k_refs])

    @pl.when(g == 0)
    def _():
        s_scr[...] = jnp.zeros(s_scr.shape, F32)
        e_scr[...] = jnp.zeros(e_scr.shape, BF16)
        m_scr[...] = jnp.zeros(m_scr.shape, F32)
        l_scr[...] = jnp.ones(l_scr.shape, F32)

    lane = lax.broadcasted_iota(jnp.int32, (tq, LANES), 1)
    zero = jnp.zeros((tq, LANES), BF16)
    q = q_ref[0] * (DIFF_HD ** -0.5)
    qs = jnp.concatenate([jnp.where(lane < DIFF_HD, q, zero), jnp.where(lane >= DIFF_HD, q, zero)], axis=0)
    l_fin = l_scr[...]
    acc = None
    n_rb = 2 * tq // DIFF_ROW_BLOCK
    for ci, (seg, r0, c0, w) in enumerate(chunks):
        first, last = ci == 0, ci == len(chunks) - 1
        pv = _dot(e_scr[:, c0:c0 + w], v_refs[seg][0, r0:r0 + w, :])
        acc = pv if acc is None else acc + pv
        for rb in range(n_rb):
            rows = slice(rb * DIFF_ROW_BLOCK, (rb + 1) * DIFF_ROW_BLOCK)
            m_prev = m_scr[rows, :]
            l_run = None if first else la_scr[rows, :]
            for j in range(w // LANES):
                cols = slice(c0 + j * LANES, c0 + (j + 1) * LANES)
                ej = jnp.exp(s_scr[rows, cols] - m_prev)
                l_run = ej if l_run is None else l_run + ej
                e_scr[rows, cols] = ej.astype(BF16)
            if last:
                l_scr[rows, :] = jnp.broadcast_to(jnp.sum(l_run, axis=1, keepdims=True), l_run.shape)
            else:
                la_scr[rows, :] = l_run
        s_scr[:, c0:c0 + w] = _dot_nt(qs, k_refs[seg][0, r0:r0 + w, :])
        for rb in range(n_rb):
            rows = slice(rb * DIFF_ROW_BLOCK, (rb + 1) * DIFF_ROW_BLOCK)
            m_run = None if first else ma_scr[rows, :]
            for j in range(w // LANES):
                sj = s_scr[rows, c0 + j * LANES:c0 + (j + 1) * LANES]
                m_run = sj if m_run is None else jnp.maximum(m_run, sj)
            if last:
                m_scr[rows, :] = jnp.broadcast_to(jnp.max(m_run, axis=1, keepdims=True), m_run.shape)
            else:
                ma_scr[rows, :] = m_run

    lv = lam_ref[...]
    lam = (jnp.exp(jnp.sum(lv[0:1] * lv[1:2], axis=-1, keepdims=True))
           - jnp.exp(jnp.sum(lv[2:3] * lv[3:4], axis=-1, keepdims=True)) + lambda_init)
    o = acc / l_fin
    o = o[:tq] - lam * o[tq:]
    o = o * lax.rsqrt(jnp.mean(o * o, axis=-1, keepdims=True) + SUBLN_EPS)
    o_ref[0] = o * sub_ref[...] * (1.0 - lambda_init)


def _diff_attn_lat(lat16, ctx16, lam_vecs, subln, lambda_init, tq):
    b, s, _ = lat16.shape
    sc = ctx16.shape[1]
    nh = DIFF_HEADS
    nt = s // tq
    n_tiles = b * nh * nt

    def tile(g, lag):
        t = jnp.clip(g - lag, 0, n_tiles - 1)
        return t // (nh * nt), (t // nt) % nh, t % nt

    def q_map(g):
        bi, h, i = tile(g, 0)
        return (bi, i, h)

    def k_map(g):
        bi, h, _ = tile(g, 0)
        return (bi, 0, nh + h)

    def v_map(g):
        bi, h, _ = tile(g, 2)
        return (bi, 0, 2 * nh + h)

    def o_map(g):
        bi, h, i = tile(g, 2)
        return (bi, i, h)

    nk = sc + s
    stat = pltpu.VMEM((2 * tq, LANES), F32)
    return pl.pallas_call(
        functools.partial(_diff_pipe_kernel, lambda_init=lambda_init),
        grid=(n_tiles + 2,),
        in_specs=[
            pl.BlockSpec((1, tq, LANES), q_map),
            pl.BlockSpec((1, sc, LANES), k_map),
            pl.BlockSpec((1, s, LANES), k_map),
            pl.BlockSpec((1, sc, LANES), v_map),
            pl.BlockSpec((1, s, LANES), v_map),
            pl.BlockSpec((4, DIFF_HD), lambda g: (0, 0)),
            pl.BlockSpec((1, DIFF_VD), lambda g: (0, 0)),
        ],
        out_specs=pl.BlockSpec((1, tq, LANES), o_map),
        out_shape=jax.ShapeDtypeStruct((b, s, BRANCH_W), F32),
        scratch_shapes=[pltpu.VMEM((2 * tq, nk), F32), pltpu.VMEM((2 * tq, nk), BF16), stat, stat, stat, stat],
        compiler_params=_cparams("arbitrary"),
        name="diff_attn_lat",
    )(lat16, ctx16, lat16, ctx16, lat16, lam_vecs, subln)


NA_Q_BLK, NA_K_BLK, NA_V_BLK = 3, 4, 5


def _na_heads(q_ref, segs, o_ref):
    tq = q_ref.shape[1]
    lane = lax.broadcasted_iota(jnp.int32, (tq, LANES), 1)
    zero = jnp.zeros((tq, LANES), BF16)
    for hp in range(NA_HEADS // 2):
        q = q_ref[0, :, hp * LANES:(hp + 1) * LANES] * (NA_HD ** -0.5)
        qs = jnp.concatenate([jnp.where(lane < NA_HD, q, zero), jnp.where(lane >= NA_HD, q, zero)], axis=0)
        scores = []
        for k_of, _, bias_of in segs:
            sc = _dot_nt(qs, k_of(hp))
            if bias_of is not None:
                sc = sc + bias_of(hp)
            scores.append(sc)
        es, den = _softmax_parts(scores)
        o = None
        for (_, v_of, _), e in zip(segs, es):
            pv = _dot(e.astype(BF16), v_of(hp))
            o = pv if o is None else o + pv
        o = o / den
        o_ref[0, :, hp * LANES:(hp + 1) * LANES] = jnp.where(lane < NA_HD, o[:tq], o[tq:])


def _na_lat_kernel(q_ref, k_ref, v_ref, kc_ref, vc_ref, bias_ref, o_ref, *, rows):
    blk = pl.program_id(1)
    kstart = jnp.clip(blk * NA_ROWS_PER_BLOCK - NA_KH // 2, 0, rows - NA_KEY_ROWS)
    off = pl.multiple_of(kstart * GRID_W, GRID_W)
    nk = NA_KEY_ROWS * GRID_W
    pair = lambda hp: slice(hp * LANES, (hp + 1) * LANES)
    window = (lambda hp: k_ref[0, pl.ds(off, nk), pair(hp)],
              lambda hp: v_ref[0, pl.ds(off, nk), pair(hp)],
              lambda hp: jnp.concatenate([bias_ref[0, 2 * hp], bias_ref[0, 2 * hp + 1]], axis=0))
    context = (lambda hp: kc_ref[0, :, pair(hp)], lambda hp: vc_ref[0, :, pair(hp)], None)
    _na_heads(q_ref, [window, context], o_ref)


def _na_lat(lat16, ctx16, bias, layer):
    b, s, _ = lat16.shape
    sc = ctx16.shape[1]
    rows = s // GRID_W
    nblk = rows // NA_ROWS_PER_BLOCK
    tq = NA_ROWS_PER_BLOCK * GRID_W
    nk = NA_KEY_ROWS * GRID_W
    bw = BRANCH_W

    def bias_map(bi, blk):
        case = jnp.where(blk == 0, 0, jnp.where(blk == nblk - 1, 2, 1))
        return (layer, case, 0, 0, 0)

    return pl.pallas_call(
        functools.partial(_na_lat_kernel, rows=rows),
        grid=(b, nblk),
        in_specs=[
            pl.BlockSpec((1, tq, bw), lambda bi, blk: (bi, blk, NA_Q_BLK)),
            pl.BlockSpec((1, s, bw), lambda bi, blk: (bi, 0, NA_K_BLK)),
            pl.BlockSpec((1, s, bw), lambda bi, blk: (bi, 0, NA_V_BLK)),
            pl.BlockSpec((1, sc, bw), lambda bi, blk: (bi, 0, NA_K_BLK)),
            pl.BlockSpec((1, sc, bw), lambda bi, blk: (bi, 0, NA_V_BLK)),
            pl.BlockSpec((None, 1, NA_HEADS, tq, nk), bias_map),
        ],
        out_specs=pl.BlockSpec((1, tq, bw), lambda bi, blk: (bi, blk, 0)),
        out_shape=jax.ShapeDtypeStruct((b, s, bw), F32),
        compiler_params=_cparams("arbitrary", "arbitrary"),
        name="na_lat",
    )(lat16, lat16, lat16, ctx16, ctx16, bias)


def _na_ctx_kernel(q_ref, k_ref, v_ref, o_ref):
    pair = lambda hp: slice(hp * LANES, (hp + 1) * LANES)
    _na_heads(q_ref, [(lambda hp: k_ref[0, :, pair(hp)], lambda hp: v_ref[0, :, pair(hp)], None)], o_ref)


def _na_ctx(ctx16):
    b, sc, _ = ctx16.shape
    bw = BRANCH_W
    return pl.pallas_call(
        _na_ctx_kernel,
        grid=(b,),
        in_specs=[
            pl.BlockSpec((1, sc, bw), lambda bi: (bi, 0, NA_Q_BLK)),
            pl.BlockSpec((1, sc, bw), lambda bi: (bi, 0, NA_K_BLK)),
            pl.BlockSpec((1, sc, bw), lambda bi: (bi, 0, NA_V_BLK)),
        ],
        out_specs=pl.BlockSpec((1, sc, bw), lambda bi: (bi, 0, 0)),
        out_shape=jax.ShapeDtypeStruct((b, sc, bw), F32),
        compiler_params=_cparams("arbitrary"),
        name="na_ctx",
    )(ctx16, ctx16, ctx16)


NA_BIAS_ROWS = 2 * NA_KH - 1
NA_BIAS_COLS = 2 * NA_KW - 1


def _na_bias_kernel(tab_ref, o_ref, e_scr, *, rows):
    l, h, case = pl.program_id(0), pl.program_id(1), pl.program_id(2)
    w = GRID_W

    @pl.when(case == 0)
    def _():
        c = lax.broadcasted_iota(jnp.int32, (w, 2 * w), 0)
        kc = lax.broadcasted_iota(jnp.int32, (w, 2 * w), 1) % w
        cstart = jnp.clip(c - NA_KW // 2, 0, w - NA_KW)
        col_ok = (kc >= cstart) & (kc < cstart + NA_KW)
        d = kc - c + (NA_KW - 1)
        base = (l * NA_HEADS + h) * (NA_BIAS_ROWS * NA_BIAS_COLS)
        for ro in range(NA_BIAS_ROWS):
            e = jnp.full((w, 2 * w), MASK_VALUE, F32)
            for t in range(NA_BIAS_COLS):
                e = jnp.where(col_ok & (d == t), tab_ref[base + ro * NA_BIAS_COLS + t], e)
            e_scr[ro] = e

    r_blk = NA_ROWS_PER_BLOCK
    r0 = jnp.where(case == 0, 0, jnp.where(case == 1, r_blk, rows - r_blk))
    kstart = jnp.where(case == 0, 0, jnp.where(case == 1, r_blk - NA_KH // 2, rows - NA_KEY_ROWS))
    lane = lax.broadcasted_iota(jnp.int32, (w, 2 * w), 1)
    masked = jnp.full((w, 2 * w), MASK_VALUE, F32)
    n_pairs = (NA_KEY_ROWS + 1) // 2
    for qi in range(r_blk):
        r = r0 + qi
        rs = jnp.clip(r - NA_KH // 2, 0, rows - NA_KH)
        pieces = []
        for kp in range(n_pairs):
            halves = []
            for ki in (2 * kp, 2 * kp + 1):
                kr = kstart + ki
                valid = (kr >= rs) & (kr < rs + NA_KH) & (ki < NA_KEY_ROWS)
                ro = jnp.clip(kr - r + NA_KH - 1, 0, NA_BIAS_ROWS - 1)
                halves.append(jnp.where(valid, e_scr[ro], masked))
            pieces.append(jnp.where(lane < w, halves[0], halves[1]))
        strip = jnp.concatenate(pieces, axis=1)
        o_ref[0, 0, 0, qi * w:(qi + 1) * w, :] = strip[:, :NA_KEY_ROWS * w]


def _na_bias(tab, rows):
    n_l = tab.shape[0]
    tq = NA_ROWS_PER_BLOCK * GRID_W
    nk = NA_KEY_ROWS * GRID_W
    return pl.pallas_call(
        functools.partial(_na_bias_kernel, rows=rows),
        grid_spec=pltpu.PrefetchScalarGridSpec(
            num_scalar_prefetch=1,
            grid=(n_l, NA_HEADS, 3),
            in_specs=[],
            out_specs=pl.BlockSpec((1, 1, 1, tq, nk), lambda l, h, case, tab_ref: (l, case, h, 0, 0)),
            scratch_shapes=[pltpu.VMEM((NA_BIAS_ROWS, GRID_W, 2 * GRID_W), F32)],
        ),
        out_shape=jax.ShapeDtypeStruct((n_l, 3, NA_HEADS, tq, nk), F32),
        compiler_params=_cparams("arbitrary", "arbitrary", "arbitrary"),
        name="na_bias",
    )(tab.reshape(-1))


def _final_kernel(x_ref, mod_ref, yf_ref, yd_ref, yp_ref, yn_ref, wg_ref, bg_ref, wb_ref, wo_ref,
                  g_ref, b_ref, o_ref, *, alpha):
    x = x_ref[0]
    xm = _modulated_ln(x, mod_ref).astype(BF16)
    bw = BRANCH_W
    d = D_MODEL
    acc = None
    for i, y_ref in enumerate((yf_ref, yd_ref, yp_ref, yn_ref)):
        zg = _dot(xm, wg_ref[:, i * bw:(i + 1) * bw]) + bg_ref[:, i * bw:(i + 1) * bw]
        gated = (y_ref[0] * (zg * _sigmoid(zg))).astype(BF16)
        proj = _dot(gated, wb_ref[i])
        c0 = N_BRANCH * bw + i * d
        zm = _dot(xm, wg_ref[:, c0:c0 + d]) + bg_ref[:, c0:c0 + d]
        term = _sigmoid(zm) * proj
        acc = term if acc is None else acc + term
    out = _dot(acc.astype(BF16), wo_ref[...])
    h = alpha * x + mod_ref[0, 2:3, :] * out
    mu = jnp.mean(h, axis=-1, keepdims=True)
    hc = h - mu
    var = jnp.mean(hc * hc, axis=-1, keepdims=True)
    o_ref[0] = hc * lax.rsqrt(var + LN_EPS) * g_ref[...] + b_ref[...]


def _final(x, mod, per_batch_mod, ys, layer, wg, bg, wb, wo, ln_g, ln_b, alpha, tm):
    b, s, d = x.shape
    ng = wg.shape[2]
    mod_map = (lambda bi, i: (bi, 0, 0)) if per_batch_mod else (lambda bi, i: (0, 0, 0))
    tile = lambda w: pl.BlockSpec((1, tm, w), lambda bi, i: (bi, i, 0))
    const2 = lambda shape: pl.BlockSpec(shape, lambda bi, i: (0, 0), pipeline_mode=pl.Buffered(1))
    layer3 = lambda shape: pl.BlockSpec((None,) + shape, lambda bi, i: (layer, 0, 0), pipeline_mode=pl.Buffered(1))
    return pl.pallas_call(
        functools.partial(_final_kernel, alpha=alpha),
        grid=(b, s // tm),
        in_specs=[
            tile(d),
            pl.BlockSpec((1, 3, d), mod_map),
            tile(BRANCH_W), tile(BRANCH_W), tile(BRANCH_W), tile(BRANCH_W),
            layer3((d, ng)),
            const2((1, ng)),
            pl.BlockSpec((None, N_BRANCH, BRANCH_W, d), lambda bi, i: (layer, 0, 0, 0), pipeline_mode=pl.Buffered(1)),
            layer3((d, d)),
            const2((1, d)),
            const2((1, d)),
        ],
        out_specs=tile(d),
        out_shape=jax.ShapeDtypeStruct((b, s, d), F32),
        compiler_params=_cparams("arbitrary", "arbitrary"),
        name="gate_merge_out",
    )(x, mod, *ys, wg, bg, wb, wo, ln_g, ln_b)


def _rope_tables(s):
    t = jnp.arange(s)
    rows = (t // GRID_W).astype(F32)
    cols = (t % GRID_W).astype(F32)
    nf = DIFF_HD // 4
    inv = ROPE_BASE ** (-jnp.arange(nf, dtype=F32) / nf)
    ang = jnp.stack([rows[:, None] * inv, cols[:, None] * inv], axis=1)
    cos = jnp.cos(ang)
    sin = jnp.sin(ang)
    cos64 = jnp.concatenate([cos[:, 0], cos[:, 0], cos[:, 1], cos[:, 1]], axis=-1)
    sin64 = jnp.concatenate([-sin[:, 0], sin[:, 0], -sin[:, 1], sin[:, 1]], axis=-1)
    rep = LANES // DIFF_HD
    return jnp.tile(cos64, (1, rep)), jnp.tile(sin64, (1, rep))


def _dft_tables(n):
    idx = jnp.arange(n, dtype=jnp.int32)
    m = (idx[:, None] * idx[None, :]) % n
    ang = m.astype(F32) * (2.0 * math.pi / n)
    return jnp.cos(ang), jnp.sin(ang)


def _split_cols(w, sizes):
    out, acc = [], 0
    for sz in sizes:
        out.append(w[..., acc:acc + sz])
        acc += sz
    return out


def kernel(x, c, ctx, c_ctx, w_mod, b_mod, w_in, b_in, fnet_w, diff_lam, diff_subln, pool_w, pool_scale,
           na_bias, w_branch, w_out, ln_g, ln_b):
    bsz, s, d = x.shape
    s_ctx = ctx.shape[1]
    rows = s // GRID_W
    assert d == D_MODEL and s % (GRID_W * NA_ROWS_PER_BLOCK) == 0 and rows >= NA_KEY_ROWS
    alpha = (2.0 * DEPTH) ** 0.25

    blocks_a = (0, 6, 2, 3, 4, 8, 9, 10)
    blocks_g = (1, 5, 7, 11) + tuple(range(12, 12 + N_BRANCH * D_MODEL // BRANCH_W))
    wa = _gather_cast_cols(w_in, blocks_a)
    wg = _gather_cast_cols(w_in, blocks_g)
    zb = _split_cols(b_in, (BRANCH_W,) * (PROJ_W // BRANCH_W))
    ba = jnp.concatenate([zb[i] for i in blocks_a], axis=-1)
    bg = jnp.concatenate([zb[i] for i in blocks_g], axis=-1)
    wb16 = w_branch.astype(BF16)
    wo16 = w_out.astype(BF16)
    fw16 = fnet_w.astype(BF16)
    pw16 = pool_w.astype(BF16)

    rope_tabs = _rope_tables(s)
    fft_tabs = {n: _fft_tables(n) for n in {s, s_ctx}}
    cc, sc_ = _dft_tables(FNET_GW)
    ccsc = jnp.concatenate([cc, sc_], axis=0).astype(BF16)
    na_bias_all = _na_bias(na_bias, rows)

    r_pad = -(-(bsz + 1) // 8) * 8
    cc_all = jnp.zeros((r_pad, d), F32).at[:bsz].set(c).at[bsz].set(c_ctx)
    mod_all = _modulation(cc_all, w_mod, b_mod)

    tm_lat = min(512, s)
    tm_fin = min(512, s)
    tq_diff = min(512, s)

    h, hc = x, ctx
    for l in range(DEPTH):
        lambda_init = 0.8 - 0.6 * math.exp(-0.3 * l)
        need_ctx = l < DEPTH - 1
        mod_lat = mod_all[l, :bsz].reshape(bsz, 3, d)
        mod_ctx = mod_all[l, bsz:bsz + 1].reshape(1, 3, d)

        lat32, lat16 = _inproj(h, mod_lat, True, wa, l, ba[l][None], rope_tabs, tm_lat)
        ctx32, ctx16 = _inproj(hc, mod_ctx, False, wa, l, ba[l][None], None, s_ctx)

        lam = diff_lam[l]
        sub = diff_subln[l][None]
        psc = pool_scale[l][None]
        y_f = _fnet_fft(lat32, fft_tabs[s], ccsc, fw16[l])
        y_d = _diff_attn_lat(lat16, ctx16, lam, sub, lambda_init, tq_diff)
        y_p = _pool(lat32, pw16[l], psc)
        y_n = _na_lat(lat16, ctx16, na_bias_all, l)
        h_new = _final(h, mod_lat, True, (y_f, y_d, y_p, y_n), l, wg, bg[l][None], wb16, wo16,
                       ln_g[l][None], ln_b[l][None], alpha, tm_fin)
        if need_ctx:
            y_fc = _fnet_fft(ctx32, fft_tabs[s_ctx], ccsc, fw16[l])
            y_dc = _diff_attn(ctx16, [ctx16], lam, sub, lambda_init, s_ctx)
            y_pc = _pool(ctx32, pw16[l], psc)
            y_nc = _na_ctx(ctx16)
            hc = _final(hc, mod_ctx, False, (y_fc, y_dc, y_pc, y_nc), l, wg, bg[l][None], wb16, wo16,
                        ln_g[l][None], ln_b[l][None], alpha, s_ctx)
        h = h_new
    return h
```

```python
import functools
import math

import numpy as np
import jax
import jax.numpy as jnp
from jax import lax
from jax.experimental import pallas as pl
from jax.experimental.pallas import tpu as pltpu

D_MODEL = 1024
DEPTH = 4
GRID_W = 64
N_BRANCH = 4
BRANCH_W = 512
FNET_GROUPS = 4
FNET_GW = BRANCH_W // FNET_GROUPS
DIFF_HEADS = 4
DIFF_HD = 64
DIFF_VD = 2 * DIFF_HD
POOL_WINDOWS = (2, 4, 8, 16)
POOL_GW = BRANCH_W // len(POOL_WINDOWS)
NA_HEADS = 8
NA_HD = BRANCH_W // NA_HEADS
NA_KH = 8
NA_KW = 16
ROPE_BASE = 10000.0
LN_EPS = 1e-6
SUBLN_EPS = 1e-5
PROJ_W = 12 * BRANCH_W + N_BRANCH * D_MODEL

LANES = 128
POOL_PAD = 16
NA_ROWS_PER_BLOCK = 4
NA_KEY_ROWS = NA_ROWS_PER_BLOCK + NA_KH - 1
MASK_VALUE = -1e30
VMEM_LIMIT = 56 * 1024 * 1024

F32 = jnp.float32
BF16 = jnp.bfloat16


def _cparams(*sem):
    return pltpu.CompilerParams(dimension_semantics=sem, vmem_limit_bytes=VMEM_LIMIT)


def _dot(a, b):
    return jnp.dot(a, b, preferred_element_type=F32)


def _dot_nt(a, b):
    return lax.dot_general(a, b, (((1,), (1,)), ((), ())), preferred_element_type=F32)


def _sigmoid(x):
    return 1.0 / (1.0 + jnp.exp(-x))


def _modulated_ln(x, mod_ref):
    mu = jnp.mean(x, axis=-1, keepdims=True)
    xc = x - mu
    var = jnp.mean(xc * xc, axis=-1, keepdims=True)
    y = xc * lax.rsqrt(var + LN_EPS)
    return y * (1.0 + mod_ref[0, 1:2, :]) + mod_ref[0, 0:1, :]


def _mod_kernel(c_ref, w_ref, b_ref, o_ref):
    c = c_ref[...]
    s = (c * _sigmoid(c)).astype(BF16)
    o_ref[0] = _dot(s, w_ref[0].astype(BF16)) + b_ref[0]


def _modulation(cc, w_mod, b_mod):
    n_l, d, d3 = w_mod.shape
    r = cc.shape[0]
    tn = d
    return pl.pallas_call(
        _mod_kernel,
        grid=(n_l, d3 // tn),
        in_specs=[
            pl.BlockSpec((r, d), lambda l, j: (0, 0)),
            pl.BlockSpec((1, d, tn), lambda l, j: (l, 0, j)),
            pl.BlockSpec((1, 1, tn), lambda l, j: (l, 0, j)),
        ],
        out_specs=pl.BlockSpec((1, r, tn), lambda l, j: (l, 0, j)),
        out_shape=jax.ShapeDtypeStruct((n_l, r, d3), F32),
        compiler_params=_cparams("arbitrary", "arbitrary"),
        name="modulation",
    )(cc, w_mod, b_mod.reshape(n_l, 1, d3))


def _cast_kernel(w_ref, o_ref):
    o_ref[...] = w_ref[...].astype(BF16)


def _gather_cast_cols(w, col_blocks):
    n_l, d, _ = w.shape
    nb = len(col_blocks)

    def src_map(l, j):
        idx = jnp.int32(col_blocks[0])
        for k, cb in enumerate(col_blocks[1:], start=1):
            idx = jnp.where(j == k, cb, idx)
        return (l, 0, idx)

    return pl.pallas_call(
        _cast_kernel,
        grid=(n_l, nb),
        in_specs=[pl.BlockSpec((1, d, BRANCH_W), src_map)],
        out_specs=pl.BlockSpec((1, d, BRANCH_W), lambda l, j: (l, 0, j)),
        out_shape=jax.ShapeDtypeStruct((n_l, d, nb * BRANCH_W), BF16),
        compiler_params=_cparams("arbitrary", "arbitrary"),
        name="gather_cast_cols",
    )(w)


def _rope(r, cos_ref, sin_ref):
    lane = lax.broadcasted_iota(jnp.int32, (r.shape[0], LANES), 1)
    first = (lane % 32) < 16
    cos = cos_ref[...]
    sin = sin_ref[...]
    outs = []
    for k in range(r.shape[1] // LANES):
        xk = r[:, k * LANES:(k + 1) * LANES]
        partner = jnp.where(first, pltpu.roll(xk, LANES - 16, 1), pltpu.roll(xk, 16, 1))
        outs.append(xk * cos + partner * sin)
    return jnp.concatenate(outs, axis=1)


def _inproj_kernel(*refs, rope):
    if rope:
        x_ref, mod_ref, w_ref, b_ref, cos_ref, sin_ref, o32_ref, o16_ref = refs
    else:
        x_ref, mod_ref, w_ref, b_ref, o32_ref, o16_ref = refs
    xm = _modulated_ln(x_ref[0], mod_ref).astype(BF16)
    bw = BRANCH_W
    for j in range(2):
        o32_ref[0, :, j * bw:(j + 1) * bw] = _dot(xm, w_ref[:, j * bw:(j + 1) * bw]) + b_ref[:, j * bw:(j + 1) * bw]
    for j in range(6):
        c0 = (2 + j) * bw
        r = _dot(xm, w_ref[:, c0:c0 + bw]) + b_ref[:, c0:c0 + bw]
        if rope and j in (0, 1):
            r = _rope(r, cos_ref, sin_ref)
        o16_ref[0, :, j * bw:(j + 1) * bw] = r.astype(BF16)


def _inproj(x, mod, per_batch_mod, wa, layer, ba, rope_tabs, tm):
    b, s, d = x.shape
    na = wa.shape[2]
    mod_map = (lambda bi, i: (bi, 0, 0)) if per_batch_mod else (lambda bi, i: (0, 0, 0))
    in_specs = [
        pl.BlockSpec((1, tm, d), lambda bi, i: (bi, i, 0)),
        pl.BlockSpec((1, 3, d), mod_map),
        pl.BlockSpec((None, d, na), lambda bi, i: (layer, 0, 0)),
        pl.BlockSpec((1, na), lambda bi, i: (0, 0)),
    ]
    args = [x, mod, wa, ba]
    if rope_tabs is not None:
        in_specs += [pl.BlockSpec((tm, LANES), lambda bi, i: (i, 0))] * 2
        args += list(rope_tabs)
    return pl.pallas_call(
        functools.partial(_inproj_kernel, rope=rope_tabs is not None),
        grid=(b, s // tm),
        in_specs=in_specs,
        out_specs=[
            pl.BlockSpec((1, tm, 2 * BRANCH_W), lambda bi, i: (bi, i, 0)),
            pl.BlockSpec((1, tm, 6 * BRANCH_W), lambda bi, i: (bi, i, 0)),
        ],
        out_shape=[
            jax.ShapeDtypeStruct((b, s, 2 * BRANCH_W), F32),
            jax.ShapeDtypeStruct((b, s, 6 * BRANCH_W), BF16),
        ],
        compiler_params=_cparams("arbitrary", "arbitrary"),
        name="inproj_rope" if rope_tabs is not None else "inproj",
    )(*args)


FFT_PAD = 4
FFT_GROUPS_PER_STEP = 2
FFT_UNROLL = 8


def _fft_kernel(u_ref, f1_ref, f2_ref, tw_ref, cc_ref, w_ref, o_ref, t_scr, z_scr, *, r, scale):
    p = r + FFT_PAD
    ng = FFT_GROUPS_PER_STEP
    lanes = lambda g: slice(g * LANES, (g + 1) * LANES)

    def transpose_in(n1, c):
        blk = u_ref[0, pl.ds(pl.multiple_of(n1 * r, r), r), :]
        for g in range(ng):
            t_scr[g, pl.ds(n1, r, stride=p), :] = blk[:, lanes(g)]
        return c

    lax.fori_loop(0, r, transpose_in, 0, unroll=FFT_UNROLL)

    def stage1(n2, c):
        m = jnp.concatenate([t_scr[g, pl.ds(n2 * p, r), :] for g in range(ng)], axis=1).astype(BF16)
        y = _dot(f1_ref[...], m)
        tr = tw_ref[n2, 0]
        ti = tw_ref[n2, 1]
        for g in range(ng):
            a = y[:r, lanes(g)]
            b = y[r:, lanes(g)]
            z_scr[0, g, pl.ds(n2, r, stride=p), :] = a * tr - b * ti
            z_scr[1, g, pl.ds(n2, r, stride=p), :] = a * ti + b * tr
        return c

    lax.fori_loop(0, r, stage1, 0, unroll=FFT_UNROLL)

    def stage2(blk, c):
        xs = []
        for i in range(FFT_UNROLL):
            k1 = blk * FFT_UNROLL + i
            zz = jnp.concatenate(
                [jnp.concatenate([z_scr[ri, g, pl.ds(k1 * p, r), :] for g in range(ng)], axis=1) for ri in range(2)],
                axis=0).astype(BF16)
            xs.append(_dot(f2_ref[...], zz))
        for g in range(ng):
            xg = jnp.concatenate([jnp.concatenate([x[:r, lanes(g)], x[r:, lanes(g)]], axis=1) for x in xs], axis=0)
            f = _dot(xg.astype(BF16), cc_ref[...]) * scale
            y = _dot(f.astype(BF16), w_ref[g])
            for i in range(FFT_UNROLL):
                t_scr[g, pl.ds(blk * FFT_UNROLL + i, r, stride=p), :] = y[i * r:(i + 1) * r]
        return c

    lax.fori_loop(0, r // FFT_UNROLL, stage2, 0)

    def copy_out(k2, c):
        for g in range(ng):
            o_ref[0, pl.ds(pl.multiple_of(k2 * r, r), r), lanes(g)] = t_scr[g, pl.ds(k2 * p, r), :]
        return c

    lax.fori_loop(0, r, copy_out, 0, unroll=FFT_UNROLL)


def _fft_tables(s):
    r = math.isqrt(s)
    assert r * r == s and r % FFT_UNROLL == 0
    cr, sr = _dft_tables(r)
    f1 = jnp.concatenate([cr, -sr], axis=0).astype(BF16)
    f2 = jnp.concatenate([jnp.concatenate([cr, sr], axis=1), jnp.concatenate([-sr, cr], axis=1)], axis=0).astype(BF16)
    idx = jnp.arange(r, dtype=jnp.int32)
    ang = (idx[:, None] * idx[None, :]).astype(F32) * (2.0 * math.pi / s)
    tw = jnp.stack([jnp.cos(ang), -jnp.sin(ang)], axis=1)
    tw = jnp.broadcast_to(tw[..., None], tw.shape + (LANES,))
    return f1, f2, tw


def _fnet_fft(o32, tabs, ccsc, w):
    b, s, _ = o32.shape
    f1, f2, tw = tabs
    r = f1.shape[1]
    p = r + FFT_PAD
    ng = FFT_GROUPS_PER_STEP
    scale = 1.0 / math.sqrt(s * FNET_GW)
    return pl.pallas_call(
        functools.partial(_fft_kernel, r=r, scale=scale),
        grid=(b, FNET_GROUPS // ng),
        in_specs=[
            pl.BlockSpec((1, s, ng * LANES), lambda bi, j: (bi, 0, j)),
            pl.BlockSpec((2 * r, r), lambda bi, j: (0, 0)),
            pl.BlockSpec((2 * r, 2 * r), lambda bi, j: (0, 0)),
            pl.BlockSpec((r, 2, r, LANES), lambda bi, j: (0, 0, 0, 0)),
            pl.BlockSpec((2 * FNET_GW, FNET_GW), lambda bi, j: (0, 0)),
            pl.BlockSpec((ng, FNET_GW, FNET_GW), lambda bi, j: (j, 0, 0)),
        ],
        out_specs=pl.BlockSpec((1, s, ng * LANES), lambda bi, j: (bi, 0, j)),
        out_shape=jax.ShapeDtypeStruct((b, s, BRANCH_W), F32),
        scratch_shapes=[pltpu.VMEM((ng, r * p, LANES), F32), pltpu.VMEM((2, ng, r * p, LANES), F32)],
        compiler_params=_cparams("arbitrary", "arbitrary"),
        name="fnet_fft",
    )(o32, f1, f2, tw, ccsc, w)


def _pool_kernel(u_ref, w_ref, sc_ref, o_ref, pad_ref):
    s = u_ref.shape[1]
    gw = POOL_GW
    t = lax.broadcasted_iota(jnp.int32, (s, 1), 0)
    zeros = jnp.zeros((POOL_PAD, gw), F32)
    for g, win in enumerate(POOL_WINDOWS):
        u = u_ref[0, :, g * gw:(g + 1) * gw]
        pad_ref[0:POOL_PAD, :] = zeros
        pad_ref[POOL_PAD + s:2 * POOL_PAD + s, :] = zeros
        pad_ref[POOL_PAD:POOL_PAD + s, :] = u
        acc = None
        for j in range(-(win // 2), win - win // 2):
            term = pad_ref[POOL_PAD + j:POOL_PAD + j + s, :]
            acc = term if acc is None else acc + term
        lo = jnp.clip(t - win // 2, 0, s)
        hi = jnp.clip(t - win // 2 + win, 0, s)
        cnt = (hi - lo).astype(F32)
        pooled = (acc / cnt - u).astype(BF16)
        y = _dot(pooled, w_ref[g])
        o_ref[0, :, g * gw:(g + 1) * gw] = y * sc_ref[:, g * gw:(g + 1) * gw]


def _pool(o32, w, scale):
    b, s, _ = o32.shape
    return pl.pallas_call(
        _pool_kernel,
        grid=(b,),
        in_specs=[
            pl.BlockSpec((1, s, BRANCH_W), lambda bi: (bi, 0, 1)),
            pl.BlockSpec((len(POOL_WINDOWS), POOL_GW, POOL_GW), lambda bi: (0, 0, 0)),
            pl.BlockSpec((1, BRANCH_W), lambda bi: (0, 0)),
        ],
        out_specs=pl.BlockSpec((1, s, BRANCH_W), lambda bi: (bi, 0, 0)),
        out_shape=jax.ShapeDtypeStruct((b, s, BRANCH_W), F32),
        scratch_shapes=[pltpu.VMEM((s + 2 * POOL_PAD, POOL_GW), F32)],
        compiler_params=_cparams("arbitrary"),
        name="pool",
    )(o32, w, scale)


def _softmax_parts(scores):
    m = functools.reduce(jnp.maximum, [jnp.max(s, axis=-1, keepdims=True) for s in scores])
    es = [jnp.exp(s - m) for s in scores]
    den = functools.reduce(lambda a, c: a + c, [jnp.sum(e, axis=-1, keepdims=True) for e in es])
    return es, den


def _diff_kernel(*refs, n_seg, lambda_init):
    q_ref = refs[0]
    k_refs = refs[1:1 + n_seg]
    v_refs = refs[1 + n_seg:1 + 2 * n_seg]
    lam_ref, sub_ref, o_ref = refs[1 + 2 * n_seg:]
    lv = lam_ref[...]
    lam = (jnp.exp(jnp.sum(lv[0:1] * lv[1:2], axis=-1, keepdims=True))
           - jnp.exp(jnp.sum(lv[2:3] * lv[3:4], axis=-1, keepdims=True)) + lambda_init)
    tq = q_ref.shape[1]
    lane = lax.broadcasted_iota(jnp.int32, (tq, LANES), 1)
    zero = jnp.zeros((tq, LANES), BF16)
    q = q_ref[0] * (DIFF_HD ** -0.5)
    qs = jnp.concatenate([jnp.where(lane < DIFF_HD, q, zero), jnp.where(lane >= DIFF_HD, q, zero)], axis=0)
    es, den = _softmax_parts([_dot_nt(qs, k_ref[0]) for k_ref in k_refs])
    l1 = den[:tq]
    rho = lam * l1 / den[tq:]
    o = None
    for seg in range(n_seg):
        a = (es[seg][:tq] - rho * es[seg][tq:]).astype(BF16)
        pv = _dot(a, v_refs[seg][0])
        o = pv if o is None else o + pv
    o = o / l1
    o = o * lax.rsqrt(jnp.mean(o * o, axis=-1, keepdims=True) + SUBLN_EPS)
    o_ref[0] = o * sub_ref[...] * (1.0 - lambda_init)


def _diff_attn(q16, kv16_list, lam_vecs, subln, lambda_init, tq):
    b, sq, _ = q16.shape
    n_seg = len(kv16_list)
    nh = DIFF_HEADS
    in_specs = [pl.BlockSpec((1, tq, LANES), lambda bi, h, i: (bi, i, h))]
    in_specs += [pl.BlockSpec((1, kv.shape[1], LANES), lambda bi, h, i: (bi, 0, nh + h)) for kv in kv16_list]
    in_specs += [pl.BlockSpec((1, kv.shape[1], LANES), lambda bi, h, i: (bi, 0, 2 * nh + h)) for kv in kv16_list]
    in_specs += [
        pl.BlockSpec((4, DIFF_HD), lambda bi, h, i: (0, 0)),
        pl.BlockSpec((1, DIFF_VD), lambda bi, h, i: (0, 0)),
    ]
    return pl.pallas_call(
        functools.partial(_diff_kernel, n_seg=n_seg, lambda_init=lambda_init),
        grid=(b, nh, sq // tq),
        in_specs=in_specs,
        out_specs=pl.BlockSpec((1, tq, LANES), lambda bi, h, i: (bi, i, h)),
        out_shape=jax.ShapeDtypeStruct((b, sq, BRANCH_W), F32),
        compiler_params=_cparams("arbitrary", "arbitrary", "arbitrary"),
        name="diff_attn",
    )(q16, *kv16_list, *kv16_list, lam_vecs, subln)


DIFF_KEY_CHUNK = 512
DIFF_ROW_BLOCK = 64


def _key_chunks(seg_lens):
    chunks, col = [], 0
    for seg, n in enumerate(seg_lens):
        for r0 in range(0, n, DIFF_KEY_CHUNK):
            w = min(DIFF_KEY_CHUNK, n - r0)
            chunks.append((seg, r0, col, w))
            col += w
    return chunks


def _diff_pipe_kernel(q_ref, kc_ref, kl_ref, vc_ref, vl_ref, lam_ref, sub_ref, o_ref,
                      s_scr, e_scr, m_scr, l_scr, ma_scr, la_scr, *, lambda_init):
    g = pl.program_id(0)
    tq = q_ref.shape[1]
    k_refs = (kc_ref, kl_ref)
    v_refs = (vc_ref, vl_ref)
    chunks = _key_chunks([r.shape[1] for r in k_refs])

    @pl.when(g == 0)
    def _():
        s_scr[...] = jnp.zeros(s_scr.shape, F32)
        e_scr[...] = jnp.zeros(e_scr.shape, BF16)
        m_scr[...] = jnp.zeros(m_scr.shape, F32)
        l_scr[...] = jnp.ones(l_scr.shape, F32)

    lane = lax.broadcasted_iota(jnp.int32, (tq, LANES), 1)
    zero = jnp.zeros((tq, LANES), BF16)
    q = q_ref[0] * (DIFF_HD ** -0.5)
    qs = jnp.concatenate([jnp.where(lane < DIFF_HD, q, zero), jnp.where(lane >= DIFF_HD, q, zero)], axis=0)
    l_fin = l_scr[...]
    acc = None
    n_rb = 2 * tq // DIFF_ROW_BLOCK
    for ci, (seg, r0, c0, w) in enumerate(chunks):
        first, last = ci == 0, ci == len(chunks) - 1
        pv = _dot(e_scr[:, c0:c0 + w], v_refs[seg][0, r0:r0 + w, :])
        acc = pv if acc is None else acc + pv
        for rb in range(n_rb):
            rows = slice(rb * DIFF_ROW_BLOCK, (rb + 1) * DIFF_ROW_BLOCK)
            m_prev = m_scr[rows, :]
            l_run = None if first else la_scr[rows, :]
            for j in range(w // LANES):
                cols = slice(c0 + j * LANES, c0 + (j + 1) * LANES)
                ej = jnp.exp(s_scr[rows, cols] - m_prev)
                l_run = ej if l_run is None else l_run + ej
                e_scr[rows, cols] = ej.astype(BF16)
            if last:
                l_scr[rows, :] = jnp.broadcast_to(jnp.sum(l_run, axis=1, keepdims=True), l_run.shape)
            else:
                la_scr[rows, :] = l_run
        s_scr[:, c0:c0 + w] = _dot_nt(qs, k_refs[seg][0, r0:r0 + w, :])
        for rb in range(n_rb):
            rows = slice(rb * DIFF_ROW_BLOCK, (rb + 1) * DIFF_ROW_BLOCK)
            m_run = None if first else ma_scr[rows, :]
            for j in range(w // LANES):
                sj = s_scr[rows, c0 + j * LANES:c0 + (j + 1) * LANES]
                m_run = sj if m_run is None else jnp.maximum(m_run, sj)
            if last:
                m_scr[rows, :] = jnp.broadcast_to(jnp.max(m_run, axis=1, keepdims=True), m_run.shape)
            else:
                ma_scr[rows, :] = m_run

    lv = lam_ref[...]
    lam = (jnp.exp(jnp.sum(lv[0:1] * lv[1:2], axis=-1, keepdims=True))
           - jnp.exp(jnp.sum(lv[2:3] * lv[3:4], axis=-1, keepdims=True)) + lambda_init)
    o = acc / l_fin
    o = o[:tq] - lam * o[tq:]
    o = o * lax.rsqrt(jnp.mean(o * o, axis=-1, keepdims=True) + SUBLN_EPS)
    o_ref[0] = o * sub_ref[...] * (1.0 - lambda_init)


def _diff_attn_lat(lat16, ctx16, lam_vecs, subln, lambda_init, tq):
    b, s, _ = lat16.shape
    sc = ctx16.shape[1]
    nh = DIFF_HEADS
    nt = s // tq
    n_tiles = b * nh * nt

    def tile(g, lag):
        t = jnp.clip(g - lag, 0, n_tiles - 1)
        return t // (nh * nt), (t // nt) % nh, t % nt

    def q_map(g):
        bi, h, i = tile(g, 0)
        return (bi, i, h)

    def k_map(g):
        bi, h, _ = tile(g, 0)
        return (bi, 0, nh + h)

    def v_map(g):
        bi, h, _ = tile(g, 2)
        return (bi, 0, 2 * nh + h)

    def o_map(g):
        bi, h, i = tile(g, 2)
        return (bi, i, h)

    nk = sc + s
    stat = pltpu.VMEM((2 * tq, LANES), F32)
    return pl.pallas_call(
        functools.partial(_diff_pipe_kernel, lambda_init=lambda_init),
        grid=(n_tiles + 2,),
        in_specs=[
            pl.BlockSpec((1, tq, LANES), q_map),
            pl.BlockSpec((1, sc, LANES), k_map),
            pl.BlockSpec((1, s, LANES), k_map),
            pl.BlockSpec((1, sc, LANES), v_map),
            pl.BlockSpec((1, s, LANES), v_map),
            pl.BlockSpec((4, DIFF_HD), lambda g: (0, 0)),
            pl.BlockSpec((1, DIFF_VD), lambda g: (0, 0)),
        ],
        out_specs=pl.BlockSpec((1, tq, LANES), o_map),
        out_shape=jax.ShapeDtypeStruct((b, s, BRANCH_W), F32),
        scratch_shapes=[pltpu.VMEM((2 * tq, nk), F32), pltpu.VMEM((2 * tq, nk), BF16), stat, stat, stat, stat],
        compiler_params=_cparams("arbitrary"),
        name="diff_attn_lat",
    )(lat16, ctx16, lat16, ctx16, lat16, lam_vecs, subln)


NA_Q_BLK, NA_K_BLK, NA_V_BLK = 3, 4, 5


def _na_heads(q_ref, segs, o_ref):
    tq = q_ref.shape[1]
    lane = lax.broadcasted_iota(jnp.int32, (tq, LANES), 1)
    zero = jnp.zeros((tq, LANES), BF16)
    n_pairs = NA_HEADS // 2

    def pair_scores(hp):
        q = q_ref[0, :, hp * LANES:(hp + 1) * LANES] * (NA_HD ** -0.5)
        qs = jnp.concatenate([jnp.where(lane < NA_HD, q, zero), jnp.where(lane >= NA_HD, q, zero)], axis=0)
        scores = []
        for k_of, _, bias_of in segs:
            sc = _dot_nt(qs, k_of(hp))
            if bias_of is not None:
                sc = sc + bias_of(hp)
            scores.append(sc)
        return scores

    scores = pair_scores(0)
    for hp in range(n_pairs):
        nxt = pair_scores(hp + 1) if hp + 1 < n_pairs else None
        es, den = _softmax_parts(scores)
        o = None
        for (_, v_of, _), e in zip(segs, es):
            pv = _dot(e.astype(BF16), v_of(hp))
            o = pv if o is None else o + pv
        o = o / den
        o_ref[0, :, hp * LANES:(hp + 1) * LANES] = jnp.where(lane < NA_HD, o[:tq], o[tq:])
        scores = nxt


def _na_lat_kernel(q_ref, k_ref, v_ref, kc_ref, vc_ref, bias_ref, o_ref, *, rows):
    blk = pl.program_id(1)
    kstart = jnp.clip(blk * NA_ROWS_PER_BLOCK - NA_KH // 2, 0, rows - NA_KEY_ROWS)
    off = pl.multiple_of(kstart * GRID_W, GRID_W)
    nk = NA_KEY_ROWS * GRID_W
    pair = lambda hp: slice(hp * LANES, (hp + 1) * LANES)
    window = (lambda hp: k_ref[0, pl.ds(off, nk), pair(hp)],
              lambda hp: v_ref[0, pl.ds(off, nk), pair(hp)],
              lambda hp: jnp.concatenate([bias_ref[0, 2 * hp], bias_ref[0, 2 * hp + 1]], axis=0))
    context = (lambda hp: kc_ref[0, :, pair(hp)], lambda hp: vc_ref[0, :, pair(hp)], None)
    _na_heads(q_ref, [window, context], o_ref)


def _na_lat(lat16, ctx16, bias, layer):
    b, s, _ = lat16.shape
    sc = ctx16.shape[1]
    rows = s // GRID_W
    nblk = rows // NA_ROWS_PER_BLOCK
    tq = NA_ROWS_PER_BLOCK * GRID_W
    nk = NA_KEY_ROWS * GRID_W
    bw = BRANCH_W

    def bias_map(bi, blk):
        case = jnp.where(blk == 0, 0, jnp.where(blk == nblk - 1, 2, 1))
        return (layer, case, 0, 0, 0)

    return pl.pallas_call(
        functools.partial(_na_lat_kernel, rows=rows),
        grid=(b, nblk),
        in_specs=[
            pl.BlockSpec((1, tq, bw), lambda bi, blk: (bi, blk, NA_Q_BLK)),
            pl.BlockSpec((1, s, bw), lambda bi, blk: (bi, 0, NA_K_BLK)),
            pl.BlockSpec((1, s, bw), lambda bi, blk: (bi, 0, NA_V_BLK)),
            pl.BlockSpec((1, sc, bw), lambda bi, blk: (bi, 0, NA_K_BLK)),
            pl.BlockSpec((1, sc, bw), lambda bi, blk: (bi, 0, NA_V_BLK)),
            pl.BlockSpec((None, 1, NA_HEADS, tq, nk), bias_map),
        ],
        out_specs=pl.BlockSpec((1, tq, bw), lambda bi, blk: (bi, blk, 0)),
        out_shape=jax.ShapeDtypeStruct((b, s, bw), F32),
        compiler_params=_cparams("arbitrary", "arbitrary"),
        name="na_lat",
    )(lat16, lat16, lat16, ctx16, ctx16, bias)


def _na_ctx_kernel(q_ref, k_ref, v_ref, o_ref):
    pair = lambda hp: slice(hp * LANES, (hp + 1) * LANES)
    _na_heads(q_ref, [(lambda hp: k_ref[0, :, pair(hp)], lambda hp: v_ref[0, :, pair(hp)], None)], o_ref)


def _na_ctx(ctx16):
    b, sc, _ = ctx16.shape
    bw = BRANCH_W
    return pl.pallas_call(
        _na_ctx_kernel,
        grid=(b,),
        in_specs=[
            pl.BlockSpec((1, sc, bw), lambda bi: (bi, 0, NA_Q_BLK)),
            pl.BlockSpec((1, sc, bw), lambda bi: (bi, 0, NA_K_BLK)),
            pl.BlockSpec((1, sc, bw), lambda bi: (bi, 0, NA_V_BLK)),
        ],
        out_specs=pl.BlockSpec((1, sc, bw), lambda bi: (bi, 0, 0)),
        out_shape=jax.ShapeDtypeStruct((b, sc, bw), F32),
        compiler_params=_cparams("arbitrary"),
        name="na_ctx",
    )(ctx16, ctx16, ctx16)


NA_BIAS_ROWS = 2 * NA_KH - 1
NA_BIAS_COLS = 2 * NA_KW - 1


def _na_bias_kernel(tab_ref, o_ref, e_scr, *, rows):
    l, h, case = pl.program_id(0), pl.program_id(1), pl.program_id(2)
    w = GRID_W

    @pl.when(case == 0)
    def _():
        c = lax.broadcasted_iota(jnp.int32, (w, 2 * w), 0)
        kc = lax.broadcasted_iota(jnp.int32, (w, 2 * w), 1) % w
        cstart = jnp.clip(c - NA_KW // 2, 0, w - NA_KW)
        col_ok = (kc >= cstart) & (kc < cstart + NA_KW)
        d = kc - c + (NA_KW - 1)
        base = (l * NA_HEADS + h) * (NA_BIAS_ROWS * NA_BIAS_COLS)
        for ro in range(NA_BIAS_ROWS):
            e = jnp.full((w, 2 * w), MASK_VALUE, F32)
            for t in range(NA_BIAS_COLS):
                e = jnp.where(col_ok & (d == t), tab_ref[base + ro * NA_BIAS_COLS + t], e)
            e_scr[ro] = e

    r_blk = NA_ROWS_PER_BLOCK
    r0 = jnp.where(case == 0, 0, jnp.where(case == 1, r_blk, rows - r_blk))
    kstart = jnp.where(case == 0, 0, jnp.where(case == 1, r_blk - NA_KH // 2, rows - NA_KEY_ROWS))
    lane = lax.broadcasted_iota(jnp.int32, (w, 2 * w), 1)
    masked = jnp.full((w, 2 * w), MASK_VALUE, F32)
    n_pairs = (NA_KEY_ROWS + 1) // 2
    for qi in range(r_blk):
        r = r0 + qi
        rs = jnp.clip(r - NA_KH // 2, 0, rows - NA_KH)
        pieces = []
        for kp in range(n_pairs):
            halves = []
            for ki in (2 * kp, 2 * kp + 1):
                kr = kstart + ki
                valid = (kr >= rs) & (kr < rs + NA_KH) & (ki < NA_KEY_ROWS)
                ro = jnp.clip(kr - r + NA_KH - 1, 0, NA_BIAS_ROWS - 1)
                halves.append(jnp.where(valid, e_scr[ro], masked))
            pieces.append(jnp.where(lane < w, halves[0], halves[1]))
        strip = jnp.concatenate(pieces, axis=1)
        o_ref[0, 0, 0, qi * w:(qi + 1) * w, :] = strip[:, :NA_KEY_ROWS * w]


def _na_bias(tab, rows):
    n_l = tab.shape[0]
    tq = NA_ROWS_PER_BLOCK * GRID_W
    nk = NA_KEY_ROWS * GRID_W
    return pl.pallas_call(
        functools.partial(_na_bias_kernel, rows=rows),
        grid_spec=pltpu.PrefetchScalarGridSpec(
            num_scalar_prefetch=1,
            grid=(n_l, NA_HEADS, 3),
            in_specs=[],
            out_specs=pl.BlockSpec((1, 1, 1, tq, nk), lambda l, h, case, tab_ref: (l, case, h, 0, 0)),
            scratch_shapes=[pltpu.VMEM((NA_BIAS_ROWS, GRID_W, 2 * GRID_W), F32)],
        ),
        out_shape=jax.ShapeDtypeStruct((n_l, 3, NA_HEADS, tq, nk), F32),
        compiler_params=_cparams("arbitrary", "arbitrary", "arbitrary"),
        name="na_bias",
    )(tab.reshape(-1))


def _final_kernel(x_ref, mod_ref, yf_ref, yd_ref, yp_ref, yn_ref, wg_ref, bg_ref, wb_ref, wo_ref,
                  g_ref, b_ref, o_ref, *, alpha):
    x = x_ref[0]
    xm = _modulated_ln(x, mod_ref).astype(BF16)
    bw = BRANCH_W
    d = D_MODEL
    acc = None
    for i, y_ref in enumerate((yf_ref, yd_ref, yp_ref, yn_ref)):
        zg = _dot(xm, wg_ref[:, i * bw:(i + 1) * bw]) + bg_ref[:, i * bw:(i + 1) * bw]
        gated = (y_ref[0] * (zg * _sigmoid(zg))).astype(BF16)
        proj = _dot(gated, wb_ref[i])
        c0 = N_BRANCH * bw + i * d
        zm = _dot(xm, wg_ref[:, c0:c0 + d]) + bg_ref[:, c0:c0 + d]
        term = _sigmoid(zm) * proj
        acc = term if acc is None else acc + term
    out = _dot(acc.astype(BF16), wo_ref[...])
    h = alpha * x + mod_ref[0, 2:3, :] * out
    mu = jnp.mean(h, axis=-1, keepdims=True)
    hc = h - mu
    var = jnp.mean(hc * hc, axis=-1, keepdims=True)
    o_ref[0] = hc * lax.rsqrt(var + LN_EPS) * g_ref[...] + b_ref[...]


def _final(x, mod, per_batch_mod, ys, layer, wg, bg, wb, wo, ln_g, ln_b, alpha, tm):
    b, s, d = x.shape
    ng = wg.shape[2]
    mod_map = (lambda bi, i: (bi, 0, 0)) if per_batch_mod else (lambda bi, i: (0, 0, 0))
    tile = lambda w: pl.BlockSpec((1, tm, w), lambda bi, i: (bi, i, 0))
    const2 = lambda shape: pl.BlockSpec(shape, lambda bi, i: (0, 0), pipeline_mode=pl.Buffered(1))
    layer3 = lambda shape: pl.BlockSpec((None,) + shape, lambda bi, i: (layer, 0, 0), pipeline_mode=pl.Buffered(1))
    return pl.pallas_call(
        functools.partial(_final_kernel, alpha=alpha),
        grid=(b, s // tm),
        in_specs=[
            tile(d),
            pl.BlockSpec((1, 3, d), mod_map),
            tile(BRANCH_W), tile(BRANCH_W), tile(BRANCH_W), tile(BRANCH_W),
            layer3((d, ng)),
            const2((1, ng)),
            pl.BlockSpec((None, N_BRANCH, BRANCH_W, d), lambda bi, i: (layer, 0, 0, 0), pipeline_mode=pl.Buffered(1)),
            layer3((d, d)),
            const2((1, d)),
            const2((1, d)),
        ],
        out_specs=tile(d),
        out_shape=jax.ShapeDtypeStruct((b, s, d), F32),
        compiler_params=_cparams("arbitrary", "arbitrary"),
        name="gate_merge_out",
    )(x, mod, *ys, wg, bg, wb, wo, ln_g, ln_b)


def _rope_tables(s):
    t = jnp.arange(s)
    rows = (t // GRID_W).astype(F32)
    cols = (t % GRID_W).astype(F32)
    nf = DIFF_HD // 4
    inv = ROPE_BASE ** (-jnp.arange(nf, dtype=F32) / nf)
    ang = jnp.stack([rows[:, None] * inv, cols[:, None] * inv], axis=1)
    cos = jnp.cos(ang)
    sin = jnp.sin(ang)
    cos64 = jnp.concatenate([cos[:, 0], cos[:, 0], cos[:, 1], cos[:, 1]], axis=-1)
    sin64 = jnp.concatenate([-sin[:, 0], sin[:, 0], -sin[:, 1], sin[:, 1]], axis=-1)
    rep = LANES // DIFF_HD
    return jnp.tile(cos64, (1, rep)), jnp.tile(sin64, (1, rep))


def _dft_tables(n):
    idx = jnp.arange(n, dtype=jnp.int32)
    m = (idx[:, None] * idx[None, :]) % n
    ang = m.astype(F32) * (2.0 * math.pi / n)
    return jnp.cos(ang), jnp.sin(ang)


def _split_cols(w, sizes):
    out, acc = [], 0
    for sz in sizes:
        out.append(w[..., acc:acc + sz])
        acc += sz
    return out


def kernel(x, c, ctx, c_ctx, w_mod, b_mod, w_in, b_in, fnet_w, diff_lam, diff_subln, pool_w, pool_scale,
           na_bias, w_branch, w_out, ln_g, ln_b):
    bsz, s, d = x.shape
    s_ctx = ctx.shape[1]
    rows = s // GRID_W
    assert d == D_MODEL and s % (GRID_W * NA_ROWS_PER_BLOCK) == 0 and rows >= NA_KEY_ROWS
    alpha = (2.0 * DEPTH) ** 0.25

    blocks_a = (0, 6, 2, 3, 4, 8, 9, 10)
    blocks_g = (1, 5, 7, 11) + tuple(range(12, 12 + N_BRANCH * D_MODEL // BRANCH_W))
    wa = _gather_cast_cols(w_in, blocks_a)
    wg = _gather_cast_cols(w_in, blocks_g)
    zb = _split_cols(b_in, (BRANCH_W,) * (PROJ_W // BRANCH_W))
    ba = jnp.concatenate([zb[i] for i in blocks_a], axis=-1)
    bg = jnp.concatenate([zb[i] for i in blocks_g], axis=-1)
    wb16 = w_branch.astype(BF16)
    wo16 = w_out.astype(BF16)
    fw16 = fnet_w.astype(BF16)
    pw16 = pool_w.astype(BF16)

    rope_tabs = _rope_tables(s)
    fft_tabs = {n: _fft_tables(n) for n in {s, s_ctx}}
    cc, sc_ = _dft_tables(FNET_GW)
    ccsc = jnp.concatenate([cc, sc_], axis=0).astype(BF16)
    na_bias_all = _na_bias(na_bias, rows)

    r_pad = -(-(bsz + 1) // 8) * 8
    cc_all = jnp.zeros((r_pad, d), F32).at[:bsz].set(c).at[bsz].set(c_ctx)
    mod_all = _modulation(cc_all, w_mod, b_mod)

    tm_lat = min(512, s)
    tm_fin = min(512, s)
    tq_diff = min(512, s)

    h, hc = x, ctx
    for l in range(DEPTH):
        lambda_init = 0.8 - 0.6 * math.exp(-0.3 * l)
        need_ctx = l < DEPTH - 1
        mod_lat = mod_all[l, :bsz].reshape(bsz, 3, d)
        mod_ctx = mod_all[l, bsz:bsz + 1].reshape(1, 3, d)

        lat32, lat16 = _inproj(h, mod_lat, True, wa, l, ba[l][None], rope_tabs, tm_lat)
        ctx32, ctx16 = _inproj(hc, mod_ctx, False, wa, l, ba[l][None], None, s_ctx)

        lam = diff_lam[l]
        sub = diff_subln[l][None]
        psc = pool_scale[l][None]
        y_f = _fnet_fft(lat32, fft_tabs[s], ccsc, fw16[l])
        y_d = _diff_attn_lat(lat16, ctx16, lam, sub, lambda_init, tq_diff)
        y_p = _pool(lat32, pw16[l], psc)
        y_n = _na_lat(lat16, ctx16, na_bias_all, l)
        h_new = _final(h, mod_lat, True, (y_f, y_d, y_p, y_n), l, wg, bg[l][None], wb16, wo16,
                       ln_g[l][None], ln_b[l][None], alpha, tm_fin)
        if need_ctx:
            y_fc = _fnet_fft(ctx32, fft_tabs[s_ctx], ccsc, fw16[l])
            y_dc = _diff_attn(ctx16, [ctx16], lam, sub, lambda_init, s_ctx)
            y_pc = _pool(ctx32, pw16[l], psc)
            y_nc = _na_ctx(ctx16)
            hc = _final(hc, mod_ctx, False, (y_fc, y_dc, y_pc, y_nc), l, wg, bg[l][None], wb16, wo16,
                        ln_g[l][None], ln_b[l][None], alpha, s_ctx)
        h = h_new
    return h
```

```python
import functools
import math

import numpy as np
import jax
import jax.numpy as jnp
from jax import lax
from jax.experimental import pallas as pl
from jax.experimental.pallas import tpu as pltpu

D_MODEL = 1024
DEPTH = 4
GRID_W = 64
N_BRANCH = 4
BRANCH_W = 512
FNET_GROUPS = 4
FNET_GW = BRANCH_W // FNET_GROUPS
DIFF_HEADS = 4
DIFF_HD = 64
DIFF_VD = 2 * DIFF_HD
POOL_WINDOWS = (2, 4, 8, 16)
POOL_GW = BRANCH_W // len(POOL_WINDOWS)
NA_HEADS = 8
NA_HD = BRANCH_W // NA_HEADS
NA_KH = 8
NA_KW = 16
ROPE_BASE = 10000.0
LN_EPS = 1e-6
SUBLN_EPS = 1e-5
PROJ_W = 12 * BRANCH_W + N_BRANCH * D_MODEL

LANES = 128
POOL_PAD = 16
NA_ROWS_PER_BLOCK = 4
NA_KEY_ROWS = NA_ROWS_PER_BLOCK + NA_KH - 1
MASK_VALUE = -1e30
VMEM_LIMIT = 56 * 1024 * 1024

F32 = jnp.float32
BF16 = jnp.bfloat16


def _cparams(*sem):
    return pltpu.CompilerParams(dimension_semantics=sem, vmem_limit_bytes=VMEM_LIMIT)


def _dot(a, b):
    return jnp.dot(a, b, preferred_element_type=F32)


def _dot_nt(a, b):
    return lax.dot_general(a, b, (((1,), (1,)), ((), ())), preferred_element_type=F32)


def _sigmoid(x):
    return 1.0 / (1.0 + jnp.exp(-x))


def _modulated_ln(x, mod_ref):
    mu = jnp.mean(x, axis=-1, keepdims=True)
    xc = x - mu
    var = jnp.mean(xc * xc, axis=-1, keepdims=True)
    y = xc * lax.rsqrt(var + LN_EPS)
    return y * (1.0 + mod_ref[0, 1:2, :]) + mod_ref[0, 0:1, :]


def _mod_kernel(c_ref, w_ref, b_ref, o_ref):
    c = c_ref[...]
    s = (c * _sigmoid(c)).astype(BF16)
    o_ref[0] = _dot(s, w_ref[0].astype(BF16)) + b_ref[0]


def _modulation(cc, w_mod, b_mod):
    n_l, d, d3 = w_mod.shape
    r = cc.shape[0]
    tn = d
    return pl.pallas_call(
        _mod_kernel,
        grid=(n_l, d3 // tn),
        in_specs=[
            pl.BlockSpec((r, d), lambda l, j: (0, 0)),
            pl.BlockSpec((1, d, tn), lambda l, j: (l, 0, j)),
            pl.BlockSpec((1, 1, tn), lambda l, j: (l, 0, j)),
        ],
        out_specs=pl.BlockSpec((1, r, tn), lambda l, j: (l, 0, j)),
        out_shape=jax.ShapeDtypeStruct((n_l, r, d3), F32),
        compiler_params=_cparams("arbitrary", "arbitrary"),
        name="modulation",
    )(cc, w_mod, b_mod.reshape(n_l, 1, d3))


def _cast_kernel(w_ref, o_ref):
    o_ref[...] = w_ref[...].astype(BF16)


def _gather_cast_cols(w, col_blocks):
    n_l, d, _ = w.shape
    nb = len(col_blocks)

    def src_map(l, j):
        idx = jnp.int32(col_blocks[0])
        for k, cb in enumerate(col_blocks[1:], start=1):
            idx = jnp.where(j == k, cb, idx)
        return (l, 0, idx)

    return pl.pallas_call(
        _cast_kernel,
        grid=(n_l, nb),
        in_specs=[pl.BlockSpec((1, d, BRANCH_W), src_map)],
        out_specs=pl.BlockSpec((1, d, BRANCH_W), lambda l, j: (l, 0, j)),
        out_shape=jax.ShapeDtypeStruct((n_l, d, nb * BRANCH_W), BF16),
        compiler_params=_cparams("arbitrary", "arbitrary"),
        name="gather_cast_cols",
    )(w)


def _rope(r, cos_ref, sin_ref):
    lane = lax.broadcasted_iota(jnp.int32, (r.shape[0], LANES), 1)
    first = (lane % 32) < 16
    cos = cos_ref[...]
    sin = sin_ref[...]
    outs = []
    for k in range(r.shape[1] // LANES):
        xk = r[:, k * LANES:(k + 1) * LANES]
        partner = jnp.where(first, pltpu.roll(xk, LANES - 16, 1), pltpu.roll(xk, 16, 1))
        outs.append(xk * cos + partner * sin)
    return jnp.concatenate(outs, axis=1)


def _inproj_kernel(*refs, rope):
    if rope:
        x_ref, mod_ref, w_ref, b_ref, cos_ref, sin_ref, o32_ref, o16_ref = refs
    else:
        x_ref, mod_ref, w_ref, b_ref, o32_ref, o16_ref = refs
    xm = _modulated_ln(x_ref[0], mod_ref).astype(BF16)
    bw = BRANCH_W
    for j in range(2):
        o32_ref[0, :, j * bw:(j + 1) * bw] = _dot(xm, w_ref[:, j * bw:(j + 1) * bw]) + b_ref[:, j * bw:(j + 1) * bw]
    for j in range(6):
        c0 = (2 + j) * bw
        r = _dot(xm, w_ref[:, c0:c0 + bw]) + b_ref[:, c0:c0 + bw]
        if rope and j in (0, 1):
            r = _rope(r, cos_ref, sin_ref)
        o16_ref[0, :, j * bw:(j + 1) * bw] = r.astype(BF16)


def _inproj(x, mod, per_batch_mod, wa, layer, ba, rope_tabs, tm):
    b, s, d = x.shape
    na = wa.shape[2]
    mod_map = (lambda bi, i: (bi, 0, 0)) if per_batch_mod else (lambda bi, i: (0, 0, 0))
    in_specs = [
        pl.BlockSpec((1, tm, d), lambda bi, i: (bi, i, 0)),
        pl.BlockSpec((1, 3, d), mod_map),
        pl.BlockSpec((None, d, na), lambda bi, i: (layer, 0, 0)),
        pl.BlockSpec((1, na), lambda bi, i: (0, 0)),
    ]
    args = [x, mod, wa, ba]
    if rope_tabs is not None:
        in_specs += [pl.BlockSpec((tm, LANES), lambda bi, i: (i, 0))] * 2
        args += list(rope_tabs)
    return pl.pallas_call(
        functools.partial(_inproj_kernel, rope=rope_tabs is not None),
        grid=(b, s // tm),
        in_specs=in_specs,
        out_specs=[
            pl.BlockSpec((1, tm, 2 * BRANCH_W), lambda bi, i: (bi, i, 0)),
            pl.BlockSpec((1, tm, 6 * BRANCH_W), lambda bi, i: (bi, i, 0)),
        ],
        out_shape=[
            jax.ShapeDtypeStruct((b, s, 2 * BRANCH_W), F32),
            jax.ShapeDtypeStruct((b, s, 6 * BRANCH_W), BF16),
        ],
        compiler_params=_cparams("arbitrary", "arbitrary"),
        name="inproj_rope" if rope_tabs is not None else "inproj",
    )(*args)


FFT_PAD = 4
FFT_GROUPS_PER_STEP = 2
FFT_UNROLL = 8


def _fft_kernel(u_ref, f1_ref, f2_ref, tw_ref, cc_ref, w_ref, o_ref, t_scr, z_scr, *, r, scale):
    p = r + FFT_PAD
    ng = FFT_GROUPS_PER_STEP
    lanes = lambda g: slice(g * LANES, (g + 1) * LANES)

    def transpose_in(n1, c):
        blk = u_ref[0, pl.ds(pl.multiple_of(n1 * r, r), r), :]
        for g in range(ng):
            t_scr[g, pl.ds(n1, r, stride=p), :] = blk[:, lanes(g)]
        return c

    lax.fori_loop(0, r, transpose_in, 0, unroll=FFT_UNROLL)

    def stage1(n2, c):
        m = jnp.concatenate([t_scr[g, pl.ds(n2 * p, r), :] for g in range(ng)], axis=1).astype(BF16)
        y = _dot(f1_ref[...], m)
        tr = tw_ref[n2, 0]
        ti = tw_ref[n2, 1]
        for g in range(ng):
            a = y[:r, lanes(g)]
            b = y[r:, lanes(g)]
            z_scr[0, g, pl.ds(n2, r, stride=p), :] = a * tr - b * ti
            z_scr[1, g, pl.ds(n2, r, stride=p), :] = a * ti + b * tr
        return c

    lax.fori_loop(0, r, stage1, 0, unroll=FFT_UNROLL)

    def stage2(blk, c):
        xs = []
        for i in range(FFT_UNROLL):
            k1 = blk * FFT_UNROLL + i
            zz = jnp.concatenate(
                [jnp.concatenate([z_scr[ri, g, pl.ds(k1 * p, r), :] for g in range(ng)], axis=1) for ri in range(2)],
                axis=0).astype(BF16)
            xs.append(_dot(f2_ref[...], zz))
        for g in range(ng):
            xg = jnp.concatenate([jnp.concatenate([x[:r, lanes(g)], x[r:, lanes(g)]], axis=1) for x in xs], axis=0)
            f = _dot(xg.astype(BF16), cc_ref[...]) * scale
            y = _dot(f.astype(BF16), w_ref[g])
            for i in range(FFT_UNROLL):
                t_scr[g, pl.ds(blk * FFT_UNROLL + i, r, stride=p), :] = y[i * r:(i + 1) * r]
        return c

    lax.fori_loop(0, r // FFT_UNROLL, stage2, 0)

    def copy_out(k2, c):
        for g in range(ng):
            o_ref[0, pl.ds(pl.multiple_of(k2 * r, r), r), lanes(g)] = t_scr[g, pl.ds(k2 * p, r), :]
        return c

    lax.fori_loop(0, r, copy_out, 0, unroll=FFT_UNROLL)


def _fft_tables(s):
    r = math.isqrt(s)
    assert r * r == s and r % FFT_UNROLL == 0
    cr, sr = _dft_tables(r)
    f1 = jnp.concatenate([cr, -sr], axis=0).astype(BF16)
    f2 = jnp.concatenate([jnp.concatenate([cr, sr], axis=1), jnp.concatenate([-sr, cr], axis=1)], axis=0).astype(BF16)
    idx = jnp.arange(r, dtype=jnp.int32)
    ang = (idx[:, None] * idx[None, :]).astype(F32) * (2.0 * math.pi / s)
    tw = jnp.stack([jnp.cos(ang), -jnp.sin(ang)], axis=1)
    tw = jnp.broadcast_to(tw[..., None], tw.shape + (LANES,))
    return f1, f2, tw


def _fnet_fft(o32, tabs, ccsc, w):
    b, s, _ = o32.shape
    f1, f2, tw = tabs
    r = f1.shape[1]
    p = r + FFT_PAD
    ng = FFT_GROUPS_PER_STEP
    scale = 1.0 / math.sqrt(s * FNET_GW)
    return pl.pallas_call(
        functools.partial(_fft_kernel, r=r, scale=scale),
        grid=(b, FNET_GROUPS // ng),
        in_specs=[
            pl.BlockSpec((1, s, ng * LANES), lambda bi, j: (bi, 0, j)),
            pl.BlockSpec((2 * r, r), lambda bi, j: (0, 0)),
            pl.BlockSpec((2 * r, 2 * r), lambda bi, j: (0, 0)),
            pl.BlockSpec((r, 2, r, LANES), lambda bi, j: (0, 0, 0, 0)),
            pl.BlockSpec((2 * FNET_GW, FNET_GW), lambda bi, j: (0, 0)),
            pl.BlockSpec((ng, FNET_GW, FNET_GW), lambda bi, j: (j, 0, 0)),
        ],
        out_specs=pl.BlockSpec((1, s, ng * LANES), lambda bi, j: (bi, 0, j)),
        out_shape=jax.ShapeDtypeStruct((b, s, BRANCH_W), F32),
        scratch_shapes=[pltpu.VMEM((ng, r * p, LANES), F32), pltpu.VMEM((2, ng, r * p, LANES), F32)],
        compiler_params=_cparams("arbitrary", "arbitrary"),
        name="fnet_fft",
    )(o32, f1, f2, tw, ccsc, w)


def _pool_kernel(u_ref, w_ref, sc_ref, o_ref, pad_ref):
    s = u_ref.shape[1]
    gw = POOL_GW
    t = lax.broadcasted_iota(jnp.int32, (s, 1), 0)
    zeros = jnp.zeros((POOL_PAD, gw), F32)
    for g, win in enumerate(POOL_WINDOWS):
        u = u_ref[0, :, g * gw:(g + 1) * gw]
        pad_ref[0:POOL_PAD, :] = zeros
        pad_ref[POOL_PAD + s:2 * POOL_PAD + s, :] = zeros
        pad_ref[POOL_PAD:POOL_PAD + s, :] = u
        acc = None
        for j in range(-(win // 2), win - win // 2):
            term = pad_ref[POOL_PAD + j:POOL_PAD + j + s, :]
            acc = term if acc is None else acc + term
        lo = jnp.clip(t - win // 2, 0, s)
        hi = jnp.clip(t - win // 2 + win, 0, s)
        cnt = (hi - lo).astype(F32)
        pooled = (acc / cnt - u).astype(BF16)
        y = _dot(pooled, w_ref[g])
        o_ref[0, :, g * gw:(g + 1) * gw] = y * sc_ref[:, g * gw:(g + 1) * gw]


def _pool(o32, w, scale):
    b, s, _ = o32.shape
    return pl.pallas_call(
        _pool_kernel,
        grid=(b,),
        in_specs=[
            pl.BlockSpec((1, s, BRANCH_W), lambda bi: (bi, 0, 1)),
            pl.BlockSpec((len(POOL_WINDOWS), POOL_GW, POOL_GW), lambda bi: (0, 0, 0)),
            pl.BlockSpec((1, BRANCH_W), lambda bi: (0, 0)),
        ],
        out_specs=pl.BlockSpec((1, s, BRANCH_W), lambda bi: (bi, 0, 0)),
        out_shape=jax.ShapeDtypeStruct((b, s, BRANCH_W), F32),
        scratch_shapes=[pltpu.VMEM((s + 2 * POOL_PAD, POOL_GW), F32)],
        compiler_params=_cparams("arbitrary"),
        name="pool",
    )(o32, w, scale)


def _softmax_parts(scores):
    m = functools.reduce(jnp.maximum, [jnp.max(s, axis=-1, keepdims=True) for s in scores])
    es = [jnp.exp(s - m) for s in scores]
    den = functools.reduce(lambda a, c: a + c, [jnp.sum(e, axis=-1, keepdims=True) for e in es])
    return es, den


def _diff_kernel(*refs, n_seg, lambda_init):
    q_ref = refs[0]
    k_refs = refs[1:1 + n_seg]
    v_refs = refs[1 + n_seg:1 + 2 * n_seg]
    lam_ref, sub_ref, o_ref = refs[1 + 2 * n_seg:]
    lv = lam_ref[...]
    lam = (jnp.exp(jnp.sum(lv[0:1] * lv[1:2], axis=-1, keepdims=True))
           - jnp.exp(jnp.sum(lv[2:3] * lv[3:4], axis=-1, keepdims=True)) + lambda_init)
    tq = q_ref.shape[1]
    lane = lax.broadcasted_iota(jnp.int32, (tq, LANES), 1)
    zero = jnp.zeros((tq, LANES), BF16)
    q = q_ref[0] * (DIFF_HD ** -0.5)
    qs = jnp.concatenate([jnp.where(lane < DIFF_HD, q, zero), jnp.where(lane >= DIFF_HD, q, zero)], axis=0)
    es, den = _softmax_parts([_dot_nt(qs, k_ref[0]) for k_ref in k_refs])
    l1 = den[:tq]
    rho = lam * l1 / den[tq:]
    o = None
    for seg in range(n_seg):
        a = (es[seg][:tq] - rho * es[seg][tq:]).astype(BF16)
        pv = _dot(a, v_refs[seg][0])
        o = pv if o is None else o + pv
    o = o / l1
    o = o * lax.rsqrt(jnp.mean(o * o, axis=-1, keepdims=True) + SUBLN_EPS)
    o_ref[0] = o * sub_ref[...] * (1.0 - lambda_init)


def _diff_attn(q16, kv16_list, lam_vecs, subln, lambda_init, tq):
    b, sq, _ = q16.shape
    n_seg = len(kv16_list)
    nh = DIFF_HEADS
    in_specs = [pl.BlockSpec((1, tq, LANES), lambda bi, h, i: (bi, i, h))]
    in_specs += [pl.BlockSpec((1, kv.shape[1], LANES), lambda bi, h, i: (bi, 0, nh + h)) for kv in kv16_list]
    in_specs += [pl.BlockSpec((1, kv.shape[1], LANES), lambda bi, h, i: (bi, 0, 2 * nh + h)) for kv in kv16_list]
    in_specs += [
        pl.BlockSpec((4, DIFF_HD), lambda bi, h, i: (0, 0)),
        pl.BlockSpec((1, DIFF_VD), lambda bi, h, i: (0, 0)),
    ]
    return pl.pallas_call(
        functools.partial(_diff_kernel, n_seg=n_seg, lambda_init=lambda_init),
        grid=(b, nh, sq // tq),
        in_specs=in_specs,
        out_specs=pl.BlockSpec((1, tq, LANES), lambda bi, h, i: (bi, i, h)),
        out_shape=jax.ShapeDtypeStruct((b, sq, BRANCH_W), F32),
        compiler_params=_cparams("arbitrary", "arbitrary", "arbitrary"),
        name="diff_attn",
    )(q16, *kv16_list, *kv16_list, lam_vecs, subln)


DIFF_KEY_CHUNK = 512
DIFF_ROW_BLOCK = 64


def _key_chunks(seg_lens):
    chunks, col = [], 0
    for seg, n in enumerate(seg_lens):
        for r0 in range(0, n, DIFF_KEY_CHUNK):
            w = min(DIFF_KEY_CHUNK, n - r0)
            chunks.append((seg, r0, col, w))
            col += w
    return chunks


def _diff_pipe_kernel(q_ref, kc_ref, kl_ref, vc_ref, vl_ref, lam_ref, sub_ref, o_ref,
                      s_scr, e_scr, m_scr, l_scr, ma_scr, la_scr, *, lambda_init):
    g = pl.program_id(0)
    tq = q_ref.shape[1]
    k_refs = (kc_ref, kl_ref)
    v_refs = (vc_ref, vl_ref)
    chunks = _key_chunks([r.shape[1] for r in k_refs])

    @pl.when(g == 0)
    def _():
        s_scr[...] = jnp.zeros(s_scr.shape, F32)
        e_scr[...] = jnp.zeros(e_scr.shape, BF16)
        m_scr[...] = jnp.zeros(m_scr.shape, F32)
        l_scr[...] = jnp.ones(l_scr.shape, F32)

    lane = lax.broadcasted_iota(jnp.int32, (tq, LANES), 1)
    zero = jnp.zeros((tq, LANES), BF16)
    q = q_ref[0] * (DIFF_HD ** -0.5)
    qs = jnp.concatenate([jnp.where(lane < DIFF_HD, q, zero), jnp.where(lane >= DIFF_HD, q, zero)], axis=0)
    l_fin = l_scr[...]
    acc = None
    n_rb = 2 * tq // DIFF_ROW_BLOCK
    for ci, (seg, r0, c0, w) in enumerate(chunks):
        first, last = ci == 0, ci == len(chunks) - 1
        pv = _dot(e_scr[:, c0:c0 + w], v_refs[seg][0, r0:r0 + w, :])
        acc = pv if acc is None else acc + pv
        for rb in range(n_rb):
            rows = slice(rb * DIFF_ROW_BLOCK, (rb + 1) * DIFF_ROW_BLOCK)
            m_prev = m_scr[rows, :]
            l_run = None if first else la_scr[rows, :]
            for j in range(w // LANES):
                cols = slice(c0 + j * LANES, c0 + (j + 1) * LANES)
                ej = jnp.exp(s_scr[rows, cols] - m_prev)
                l_run = ej if l_run is None else l_run + ej
                e_scr[rows, cols] = ej.astype(BF16)
            if last:
                l_scr[rows, :] = jnp.broadcast_to(jnp.sum(l_run, axis=1, keepdims=True), l_run.shape)
            else:
                la_scr[rows, :] = l_run
        s_scr[:, c0:c0 + w] = _dot_nt(qs, k_refs[seg][0, r0:r0 + w, :])
        for rb in range(n_rb):
            rows = slice(rb * DIFF_ROW_BLOCK, (rb + 1) * DIFF_ROW_BLOCK)
            m_run = None if first else ma_scr[rows, :]
            for j in range(w // LANES):
                sj = s_scr[rows, c0 + j * LANES:c0 + (j + 1) * LANES]
                m_run = sj if m_run is None else jnp.maximum(m_run, sj)
            if last:
                m_scr[rows, :] = jnp.broadcast_to(jnp.max(m_run, axis=1, keepdims=True), m_run.shape)
            else:
                ma_scr[rows, :] = m_run

    lv = lam_ref[...]
    lam = (jnp.exp(jnp.sum(lv[0:1] * lv[1:2], axis=-1, keepdims=True))
           - jnp.exp(jnp.sum(lv[2:3] * lv[3:4], axis=-1, keepdims=True)) + lambda_init)
    o = acc / l_fin
    o = o[:tq] - lam * o[tq:]
    o = o * lax.rsqrt(jnp.mean(o * o, axis=-1, keepdims=True) + SUBLN_EPS)
    o_ref[0] = o * sub_ref[...] * (1.0 - lambda_init)


def _diff_attn_lat(lat16, ctx16, lam_vecs, subln, lambda_init, tq):
    b, s, _ = lat16.shape
    sc = ctx16.shape[1]
    nh = DIFF_HEADS
    nt = s // tq
    n_tiles = b * nh * nt

    def tile(g, lag):
        t = jnp.clip(g - lag, 0, n_tiles - 1)
        return t // (nh * nt), (t // nt) % nh, t % nt

    def q_map(g):
        bi, h, i = tile(g, 0)
        return (bi, i, h)

    def k_map(g):
        bi, h, _ = tile(g, 0)
        return (bi, 0, nh + h)

    def v_map(g):
        bi, h, _ = tile(g, 2)
        return (bi, 0, 2 * nh + h)

    def o_map(g):
        bi, h, i = tile(g, 2)
        return (bi, i, h)

    nk = sc + s
    stat = pltpu.VMEM((2 * tq, LANES), F32)
    return pl.pallas_call(
        functools.partial(_diff_pipe_kernel, lambda_init=lambda_init),
        grid=(n_tiles + 2,),
        in_specs=[
            pl.BlockSpec((1, tq, LANES), q_map),
            pl.BlockSpec((1, sc, LANES), k_map),
            pl.BlockSpec((1, s, LANES), k_map),
            pl.BlockSpec((1, sc, LANES), v_map),
            pl.BlockSpec((1, s, LANES), v_map),
            pl.BlockSpec((4, DIFF_HD), lambda g: (0, 0)),
            pl.BlockSpec((1, DIFF_VD), lambda g: (0, 0)),
        ],
        out_specs=pl.BlockSpec((1, tq, LANES), o_map),
        out_shape=jax.ShapeDtypeStruct((b, s, BRANCH_W), F32),
        scratch_shapes=[pltpu.VMEM((2 * tq, nk), F32), pltpu.VMEM((2 * tq, nk), BF16), stat, stat, stat, stat],
        compiler_params=_cparams("arbitrary"),
        name="diff_attn_lat",
    )(lat16, ctx16, lat16, ctx16, lat16, lam_vecs, subln)


NA_Q_BLK, NA_K_BLK, NA_V_BLK = 3, 4, 5


def _na_heads(q_ref, segs, o_ref):
    tq = q_ref.shape[1]
    lane = lax.broadcasted_iota(jnp.int32, (tq, LANES), 1)
    zero = jnp.zeros((tq, LANES), BF16)
    n_pairs = NA_HEADS // 2

    def pair_scores(hp):
        q = q_ref[0, :, hp * LANES:(hp + 1) * LANES] * (NA_HD ** -0.5)
        qs = jnp.concatenate([jnp.where(lane < NA_HD, q, zero), jnp.where(lane >= NA_HD, q, zero)], axis=0)
        scores = []
        for k_of, _, bias_of in segs:
            sc = _dot_nt(qs, k_of(hp))
            if bias_of is not None:
                sc = sc + bias_of(hp)
            scores.append(sc)
        return scores

    scores = pair_scores(0)
    for hp in range(n_pairs):
        nxt = pair_scores(hp + 1) if hp + 1 < n_pairs else None
        es, den = _softmax_parts(scores)
        o = None
        for (_, v_of, _), e in zip(segs, es):
            pv = _dot(e.astype(BF16), v_of(hp))
            o = pv if o is None else o + pv
        o = o / den
        o_ref[0, :, hp * LANES:(hp + 1) * LANES] = jnp.where(lane < NA_HD, o[:tq], o[tq:])
        scores = nxt


def _na_lat_kernel(q_ref, k_ref, v_ref, kc_ref, vc_ref, bias_ref, o_ref, *, rows):
    blk = pl.program_id(1)
    kstart = jnp.clip(blk * NA_ROWS_PER_BLOCK - NA_KH // 2, 0, rows - NA_KEY_ROWS)
    off = pl.multiple_of(kstart * GRID_W, GRID_W)
    nk = NA_KEY_ROWS * GRID_W
    pair = lambda hp: slice(hp * LANES, (hp + 1) * LANES)
    window = (lambda hp: k_ref[0, pl.ds(off, nk), pair(hp)],
              lambda hp: v_ref[0, pl.ds(off, nk), pair(hp)],
              lambda hp: jnp.concatenate([bias_ref[0, 2 * hp], bias_ref[0, 2 * hp + 1]], axis=0))
    context = (lambda hp: kc_ref[0, :, pair(hp)], lambda hp: vc_ref[0, :, pair(hp)], None)
    _na_heads(q_ref, [window, context], o_ref)


def _na_lat(lat16, ctx16, bias, layer):
    b, s, _ = lat16.shape
    sc = ctx16.shape[1]
    rows = s // GRID_W
    nblk = rows // NA_ROWS_PER_BLOCK
    tq = NA_ROWS_PER_BLOCK * GRID_W
    nk = NA_KEY_ROWS * GRID_W
    bw = BRANCH_W

    def bias_map(bi, blk):
        case = jnp.where(blk == 0, 0, jnp.where(blk == nblk - 1, 2, 1))
        return (layer, case, 0, 0, 0)

    return pl.pallas_call(
        functools.partial(_na_lat_kernel, rows=rows),
        grid=(b, nblk),
        in_specs=[
            pl.BlockSpec((1, tq, bw), lambda bi, blk: (bi, blk, NA_Q_BLK)),
            pl.BlockSpec((1, s, bw), lambda bi, blk: (bi, 0, NA_K_BLK)),
            pl.BlockSpec((1, s, bw), lambda bi, blk: (bi, 0, NA_V_BLK)),
            pl.BlockSpec((1, sc, bw), lambda bi, blk: (bi, 0, NA_K_BLK)),
            pl.BlockSpec((1, sc, bw), lambda bi, blk: (bi, 0, NA_V_BLK)),
            pl.BlockSpec((None, 1, NA_HEADS, tq, nk), bias_map),
        ],
        out_specs=pl.BlockSpec((1, tq, bw), lambda bi, blk: (bi, blk, 0)),
        out_shape=jax.ShapeDtypeStruct((b, s, bw), F32),
        compiler_params=_cparams("arbitrary", "arbitrary"),
        name="na_lat",
    )(lat16, lat16, lat16, ctx16, ctx16, bias)


def _na_ctx_kernel(q_ref, k_ref, v_ref, o_ref):
    pair = lambda hp: slice(hp * LANES, (hp + 1) * LANES)
    _na_heads(q_ref, [(lambda hp: k_ref[0, :, pair(hp)], lambda hp: v_ref[0, :, pair(hp)], None)], o_ref)


def _na_ctx(ctx16):
    b, sc, _ = ctx16.shape
    bw = BRANCH_W
    return pl.pallas_call(
        _na_ctx_kernel,
        grid=(b,),
        in_specs=[
            pl.BlockSpec((1, sc, bw), lambda bi: (bi, 0, NA_Q_BLK)),
            pl.BlockSpec((1, sc, bw), lambda bi: (bi, 0, NA_K_BLK)),
            pl.BlockSpec((1, sc, bw), lambda bi: (bi, 0, NA_V_BLK)),
        ],
        out_specs=pl.BlockSpec((1, sc, bw), lambda bi: (bi, 0, 0)),
        out_shape=jax.ShapeDtypeStruct((b, sc, bw), F32),
        compiler_params=_cparams("arbitrary"),
        name="na_ctx",
    )(ctx16, ctx16, ctx16)


NA_BIAS_ROWS = 2 * NA_KH - 1
NA_BIAS_COLS = 2 * NA_KW - 1


def _na_bias_kernel(tab_ref, o_ref, e_scr, *, rows):
    l, h, case = pl.program_id(0), pl.program_id(1), pl.program_id(2)
    w = GRID_W

    @pl.when(case == 0)
    def _():
        c = lax.broadcasted_iota(jnp.int32, (w, 2 * w), 0)
        kc = lax.broadcasted_iota(jnp.int32, (w, 2 * w), 1) % w
        cstart = jnp.clip(c - NA_KW // 2, 0, w - NA_KW)
        col_ok = (kc >= cstart) & (kc < cstart + NA_KW)
        d = kc - c + (NA_KW - 1)
        base = (l * NA_HEADS + h) * (NA_BIAS_ROWS * NA_BIAS_COLS)
        for ro in range(NA_BIAS_ROWS):
            e = jnp.full((w, 2 * w), MASK_VALUE, F32)
            for t in range(NA_BIAS_COLS):
                e = jnp.where(col_ok & (d == t), tab_ref[base + ro * NA_BIAS_COLS + t], e)
            e_scr[ro] = e

    r_blk = NA_ROWS_PER_BLOCK
    r0 = jnp.where(case == 0, 0, jnp.where(case == 1, r_blk, rows - r_blk))
    kstart = jnp.where(case == 0, 0, jnp.where(case == 1, r_blk - NA_KH // 2, rows - NA_KEY_ROWS))
    lane = lax.broadcasted_iota(jnp.int32, (w, 2 * w), 1)
    masked = jnp.full((w, 2 * w), MASK_VALUE, F32)
    n_pairs = (NA_KEY_ROWS + 1) // 2
    for qi in range(r_blk):
        r = r0 + qi
        rs = jnp.clip(r - NA_KH // 2, 0, rows - NA_KH)
        pieces = []
        for kp in range(n_pairs):
            halves = []
            for ki in (2 * kp, 2 * kp + 1):
                kr = kstart + ki
                valid = (kr >= rs) & (kr < rs + NA_KH) & (ki < NA_KEY_ROWS)
                ro = jnp.clip(kr - r + NA_KH - 1, 0, NA_BIAS_ROWS - 1)
                halves.append(jnp.where(valid, e_scr[ro], masked))
            pieces.append(jnp.where(lane < w, halves[0], halves[1]))
        strip = jnp.concatenate(pieces, axis=1)
        o_ref[0, 0, 0, qi * w:(qi + 1) * w, :] = strip[:, :NA_KEY_ROWS * w]


def _na_bias(tab, rows):
    n_l = tab.shape[0]
    tq = NA_ROWS_PER_BLOCK * GRID_W
    nk = NA_KEY_ROWS * GRID_W
    return pl.pallas_call(
        functools.partial(_na_bias_kernel, rows=rows),
        grid_spec=pltpu.PrefetchScalarGridSpec(
            num_scalar_prefetch=1,
            grid=(n_l, NA_HEADS, 3),
            in_specs=[],
            out_specs=pl.BlockSpec((1, 1, 1, tq, nk), lambda l, h, case, tab_ref: (l, case, h, 0, 0)),
            scratch_shapes=[pltpu.VMEM((NA_BIAS_ROWS, GRID_W, 2 * GRID_W), F32)],
        ),
        out_shape=jax.ShapeDtypeStruct((n_l, 3, NA_HEADS, tq, nk), F32),
        compiler_params=_cparams("arbitrary", "arbitrary", "arbitrary"),
        name="na_bias",
    )(tab.reshape(-1))


def _final_kernel(x_ref, mod_ref, yf_ref, yd_ref, yp_ref, yn_ref, wg_ref, bg_ref, wb_ref, wo_ref,
                  g_ref, b_ref, o_ref, *, alpha):
    x = x_ref[0]
    xm = _modulated_ln(x, mod_ref).astype(BF16)
    bw = BRANCH_W
    d = D_MODEL
    y_refs = (yf_ref, yd_ref, yp_ref, yn_ref)

    def gate_logits(i):
        c0 = N_BRANCH * bw + i * d
        zg = _dot(xm, wg_ref[:, i * bw:(i + 1) * bw]) + bg_ref[:, i * bw:(i + 1) * bw]
        zm = _dot(xm, wg_ref[:, c0:c0 + d]) + bg_ref[:, c0:c0 + d]
        return zg, zm

    acc = None
    logits = gate_logits(0)
    for i in range(N_BRANCH):
        nxt = gate_logits(i + 1) if i + 1 < N_BRANCH else None
        zg, zm = logits
        gated = (y_refs[i][0] * (zg * _sigmoid(zg))).astype(BF16)
        term = _sigmoid(zm) * _dot(gated, wb_ref[i])
        acc = term if acc is None else acc + term
        logits = nxt
    out = _dot(acc.astype(BF16), wo_ref[...])
    h = alpha * x + mod_ref[0, 2:3, :] * out
    mu = jnp.mean(h, axis=-1, keepdims=True)
    hc = h - mu
    var = jnp.mean(hc * hc, axis=-1, keepdims=True)
    o_ref[0] = hc * lax.rsqrt(var + LN_EPS) * g_ref[...] + b_ref[...]


def _final(x, mod, per_batch_mod, ys, layer, wg, bg, wb, wo, ln_g, ln_b, alpha, tm):
    b, s, d = x.shape
    ng = wg.shape[2]
    mod_map = (lambda bi, i: (bi, 0, 0)) if per_batch_mod else (lambda bi, i: (0, 0, 0))
    tile = lambda w: pl.BlockSpec((1, tm, w), lambda bi, i: (bi, i, 0))
    const2 = lambda shape: pl.BlockSpec(shape, lambda bi, i: (0, 0), pipeline_mode=pl.Buffered(1))
    layer3 = lambda shape: pl.BlockSpec((None,) + shape, lambda bi, i: (layer, 0, 0), pipeline_mode=pl.Buffered(1))
    return pl.pallas_call(
        functools.partial(_final_kernel, alpha=alpha),
        grid=(b, s // tm),
        in_specs=[
            tile(d),
            pl.BlockSpec((1, 3, d), mod_map),
            tile(BRANCH_W), tile(BRANCH_W), tile(BRANCH_W), tile(BRANCH_W),
            layer3((d, ng)),
            const2((1, ng)),
            pl.BlockSpec((None, N_BRANCH, BRANCH_W, d), lambda bi, i: (layer, 0, 0, 0), pipeline_mode=pl.Buffered(1)),
            layer3((d, d)),
            const2((1, d)),
            const2((1, d)),
        ],
        out_specs=tile(d),
        out_shape=jax.ShapeDtypeStruct((b, s, d), F32),
        compiler_params=_cparams("arbitrary", "arbitrary"),
        name="gate_merge_out",
    )(x, mod, *ys, wg, bg, wb, wo, ln_g, ln_b)


def _rope_tables(s):
    t = jnp.arange(s)
    rows = (t // GRID_W).astype(F32)
    cols = (t % GRID_W).astype(F32)
    nf = DIFF_HD // 4
    inv = ROPE_BASE ** (-jnp.arange(nf, dtype=F32) / nf)
    ang = jnp.stack([rows[:, None] * inv, cols[:, None] * inv], axis=1)
    cos = jnp.cos(ang)
    sin = jnp.sin(ang)
    cos64 = jnp.concatenate([cos[:, 0], cos[:, 0], cos[:, 1], cos[:, 1]], axis=-1)
    sin64 = jnp.concatenate([-sin[:, 0], sin[:, 0], -sin[:, 1], sin[:, 1]], axis=-1)
    rep = LANES // DIFF_HD
    return jnp.tile(cos64, (1, rep)), jnp.tile(sin64, (1, rep))


def _dft_tables(n):
    idx = jnp.arange(n, dtype=jnp.int32)
    m = (idx[:, None] * idx[None, :]) % n
    ang = m.astype(F32) * (2.0 * math.pi / n)
    return jnp.cos(ang), jnp.sin(ang)


def _split_cols(w, sizes):
    out, acc = [], 0
    for sz in sizes:
        out.append(w[..., acc:acc + sz])
        acc += sz
    return out


def kernel(x, c, ctx, c_ctx, w_mod, b_mod, w_in, b_in, fnet_w, diff_lam, diff_subln, pool_w, pool_scale,
           na_bias, w_branch, w_out, ln_g, ln_b):
    bsz, s, d = x.shape
    s_ctx = ctx.shape[1]
    rows = s // GRID_W
    assert d == D_MODEL and s % (GRID_W * NA_ROWS_PER_BLOCK) == 0 and rows >= NA_KEY_ROWS
    alpha = (2.0 * DEPTH) ** 0.25

    blocks_a = (0, 6, 2, 3, 4, 8, 9, 10)
    blocks_g = (1, 5, 7, 11) + tuple(range(12, 12 + N_BRANCH * D_MODEL // BRANCH_W))
    wa = _gather_cast_cols(w_in, blocks_a)
    wg = _gather_cast_cols(w_in, blocks_g)
    zb = _split_cols(b_in, (BRANCH_W,) * (PROJ_W // BRANCH_W))
    ba = jnp.concatenate([zb[i] for i in blocks_a], axis=-1)
    bg = jnp.concatenate([zb[i] for i in blocks_g], axis=-1)
    wb16 = w_branch.astype(BF16)
    wo16 = w_out.astype(BF16)
    fw16 = fnet_w.astype(BF16)
    pw16 = pool_w.astype(BF16)

    rope_tabs = _rope_tables(s)
    fft_tabs = {n: _fft_tables(n) for n in {s, s_ctx}}
    cc, sc_ = _dft_tables(FNET_GW)
    ccsc = jnp.concatenate([cc, sc_], axis=0).astype(BF16)
    na_bias_all = _na_bias(na_bias, rows)

    r_pad = -(-(bsz + 1) // 8) * 8
    cc_all = jnp.zeros((r_pad, d), F32).at[:bsz].set(c).at[bsz].set(c_ctx)
    mod_all = _modulation(cc_all, w_mod, b_mod)

    tm_lat = min(512, s)
    tm_fin = min(512, s)
    tq_diff = min(512, s)

    h, hc = x, ctx
    for l in range(DEPTH):
        lambda_init = 0.8 - 0.6 * math.exp(-0.3 * l)
        need_ctx = l < DEPTH - 1
        mod_lat = mod_all[l, :bsz].reshape(bsz, 3, d)
        mod_ctx = mod_all[l, bsz:bsz + 1].reshape(1, 3, d)

        lat32, lat16 = _inproj(h, mod_lat, True, wa, l, ba[l][None], rope_tabs, tm_lat)
        ctx32, ctx16 = _inproj(hc, mod_ctx, False, wa, l, ba[l][None], None, s_ctx)

        lam = diff_lam[l]
        sub = diff_subln[l][None]
        psc = pool_scale[l][None]
        y_f = _fnet_fft(lat32, fft_tabs[s], ccsc, fw16[l])
        y_d = _diff_attn_lat(lat16, ctx16, lam, sub, lambda_init, tq_diff)
        y_p = _pool(lat32, pw16[l], psc)
        y_n = _na_lat(lat16, ctx16, na_bias_all, l)
        h_new = _final(h, mod_lat, True, (y_f, y_d, y_p, y_n), l, wg, bg[l][None], wb16, wo16,
                       ln_g[l][None], ln_b[l][None], alpha, tm_fin)
        if need_ctx:
            y_fc = _fnet_fft(ctx32, fft_tabs[s_ctx], ccsc, fw16[l])
            y_dc = _diff_attn(ctx16, [ctx16], lam, sub, lambda_init, s_ctx)
            y_pc = _pool(ctx32, pw16[l], psc)
            y_nc = _na_ctx(ctx16)
            hc = _final(hc, mod_ctx, False, (y_fc, y_dc, y_pc, y_nc), l, wg, bg[l][None], wb16, wo16,
                        ln_g[l][None], ln_b[l][None], alpha, s_ctx)
        h = h_new
    return h
```
